```python
import math
import jax, jax.numpy as jnp
from jax import lax
import numpy as np


D_MODEL = 2048
BATCH = 2
SEQ = 4096
DEPTH = 2
DEC_BATCH = 128
DEC_SEQ = 4
PAST_LEN = 2048
PAGE_SIZE = 128

HEAD_DIM = 128
MIX_WIDTH = D_MODEL // 2
N_HEADS = MIX_WIDTH // HEAD_DIM
H_RET = N_HEADS
H_FOX = N_HEADS
H_SB = N_HEADS
N_BRANCH = 3
D_FF = 4 * D_MODEL
Q_BLOCK = 128
RET_CHUNK = 128
ROPE_BASE = 10000.0
FORGET_BIAS_INIT = 3.0
EPS = 1e-6
IN_SIZES = (MIX_WIDTH, MIX_WIDTH, MIX_WIDTH, MIX_WIDTH,
            MIX_WIDTH, MIX_WIDTH, MIX_WIDTH, H_FOX,
            MIX_WIDTH, MIX_WIDTH, MIX_WIDTH,
            N_BRANCH * D_MODEL)
N_IN = sum(IN_SIZES)

kernel_name = 'hybrid_retention_fox_stickbreaking_decoder'


def rmsnorm(x, g):
    xf = x.astype(jnp.float32)
    y = xf * lax.rsqrt(jnp.mean(xf * xf, axis=-1, keepdims=True) + EPS)
    return (y * g.astype(jnp.float32)).astype(x.dtype)


def split_cols(a, sizes):
    out = []
    off = 0
    for s in sizes:
        out.append(a[..., off:off + s])
        off += s
    return out


def rope(x, pos):
    half = HEAD_DIM // 2
    inv = ROPE_BASE ** (-jnp.arange(half, dtype=jnp.float32) / half)
    ang = pos.astype(jnp.float32)[:, None] * inv[None, :]
    cos = jnp.cos(ang)[None, :, None, :]
    sin = jnp.sin(ang)[None, :, None, :]
    x1 = x[..., :half].astype(jnp.float32)
    x2 = x[..., half:].astype(jnp.float32)
    return jnp.concatenate([x1 * cos - x2 * sin, x2 * cos + x1 * sin], axis=-1).astype(x.dtype)


def retention_log_gamma():
    return jnp.log(1.0 - jnp.exp2(-5.0 - jnp.arange(H_RET, dtype=jnp.float32)))


def retention(q, k, v, s0):
    bsz, t, h, _ = q.shape
    c = math.gcd(t, RET_CHUNK)
    n = t // c
    def chunks(a):
        return jnp.moveaxis(a.astype(jnp.float32).reshape(bsz, n, c, h, a.shape[-1]), 1, 0)
    lg = retention_log_gamma()
    i = jnp.arange(c, dtype=jnp.float32)
    diff = i[:, None] - i[None, :]
    inner = jnp.where(diff[None] >= 0, jnp.exp(diff[None] * lg[:, None, None]), 0.0)
    cross = jnp.exp((i[:, None] + 1.0) * lg[None, :])
    kdec = jnp.exp((c - 1.0 - i)[:, None] * lg[None, :])
    cdec = jnp.exp(c * lg)

    def step(s, qkv):
        qc, kc, vc = qkv
        a = jnp.einsum('bihd,bjhd->bhij', qc, kc) * inner[None]
        o = (jnp.einsum('bhij,bjhe->bihe', a, vc)
             + jnp.einsum('bihd,bhde->bihe', qc, s) * cross[None, :, :, None])
        s = s * cdec[None, :, None, None] + jnp.einsum('bjhd,bjhe->bhde', kc * kdec[None, :, :, None], vc)
        return s, o

    s, o = lax.scan(step, s0.astype(jnp.float32), (chunks(q), chunks(k), chunks(v)))
    o = jnp.moveaxis(o, 0, 1).reshape(bsz, t, h, v.shape[-1])
    return o, s


def pair_scores(q, k_past, k_new):
    s_past = jnp.einsum('bqhd,bkhd->bhqk', q, k_past, preferred_element_type=jnp.float32)
    s_new = jnp.einsum('bqhd,bkhd->bhqk', q, k_new, preferred_element_type=jnp.float32)
    return jnp.concatenate([s_past, s_new], axis=-1) * (HEAD_DIM ** -0.5)


def pair_values(p, v_past, v_new):
    n_past = v_past.shape[1]
    return (jnp.einsum('bhqk,bkhd->bqhd', p[..., :n_past], v_past.astype(jnp.float32))
            + jnp.einsum('bhqk,bkhd->bqhd', p[..., n_past:], v_new.astype(jnp.float32)))


def fox_attend(q, cq, qpos, k_past, v_past, k_new, v_new, ck, kpos):
    s = pair_scores(q, k_past, k_new)
    s = s + jnp.swapaxes(cq, 1, 2)[..., None] - jnp.swapaxes(ck, 1, 2)[:, :, None, :]
    s = jnp.where(kpos[None, :] <= qpos[:, None], s, -jnp.inf)
    p = jax.nn.softmax(s, axis=-1)
    return pair_values(p, v_past, v_new)


def sb_attend(q, qpos, k_past, v_past, k_new, v_new, kpos):
    z = pair_scores(q, k_past, k_new)
    valid = kpos[None, :] < qpos[:, None]
    lneg = jnp.where(valid, jax.nn.log_sigmoid(-z), 0.0)
    suffix = lax.cumsum(lneg, axis=3, reverse=True) - lneg
    a = jnp.where(valid, jnp.exp(jax.nn.log_sigmoid(z) + suffix), 0.0)
    return pair_values(a, v_past, v_new)


def sweep_queries(fn, q_args, q_axes, kv_args):
    t = q_args[0].shape[q_axes[0]]
    if t <= Q_BLOCK:
        return fn(*q_args, *kv_args)
    nb = t // Q_BLOCK
    def blk(a, ax):
        a = a.reshape(a.shape[:ax] + (nb, Q_BLOCK) + a.shape[ax + 1:])
        return jnp.moveaxis(a, ax, 0)
    blocked = tuple(blk(a, ax) for a, ax in zip(q_args, q_axes))
    out = lax.map(lambda qa: fn(*qa, *kv_args), blocked)
    out = jnp.moveaxis(out, 0, 1)
    return out.reshape((out.shape[0], t) + out.shape[3:])


def decoder_layer(x, pos0, fox_k_past, fox_v_past, fox_logf_past, sb_k_past, sb_v_past, ret_s0,
                  w_in, b_gate, b_forget, w_ret_o, w_fox_o, w_sb_o, w_out, w_up, w_down, g_mix, g_mlp):
    bsz, t, _ = x.shape
    f32 = jnp.float32
    xn = rmsnorm(x, g_mix)
    proj = xn @ w_in
    (q_r, k_r, v_r, g_r, q_f, k_f, v_f, f_logit, q_s, k_s, v_s, gate_logit) = split_cols(proj, IN_SIZES)
    def heads(a, h):
        return a.reshape(bsz, t, h, HEAD_DIM)
    pos = pos0 + jnp.arange(t)
    n_keys = fox_k_past.shape[1] + t
    kpos = jnp.arange(n_keys)

    q_r = rope(heads(q_r, H_RET), pos)
    k_r = rope(heads(k_r, H_RET), pos) * (HEAD_DIM ** -0.5)
    o_r, ret_s = retention(q_r, k_r, heads(v_r, H_RET), ret_s0)
    o_r = o_r * lax.rsqrt(jnp.mean(o_r * o_r, axis=-1, keepdims=True) + EPS)
    o_r = o_r.reshape(bsz, t, MIX_WIDTH) * jax.nn.silu(g_r.astype(f32))
    branch_r = o_r.astype(x.dtype) @ w_ret_o

    k_f = heads(k_f, H_FOX)
    v_f = heads(v_f, H_FOX)
    logf = jax.nn.log_sigmoid((f_logit + b_forget).astype(f32))
    cum = jnp.cumsum(jnp.concatenate([fox_logf_past.astype(f32), logf], axis=1), axis=1)
    o_f = sweep_queries(fox_attend, (heads(q_f, H_FOX), cum[:, n_keys - t:], pos), (1, 1, 0),
                        (fox_k_past, fox_v_past, k_f, v_f, cum, kpos))
    branch_f = o_f.reshape(bsz, t, MIX_WIDTH).astype(x.dtype) @ w_fox_o

    k_s = heads(k_s, H_SB)
    v_s = heads(v_s, H_SB)
    o_s = sweep_queries(sb_attend, (heads(q_s, H_SB), pos), (1, 0),
                        (sb_k_past, sb_v_past, k_s, v_s, kpos))
    branch_s = o_s.reshape(bsz, t, MIX_WIDTH).astype(x.dtype) @ w_sb_o

    gates = jax.nn.sigmoid((gate_logit + b_gate).astype(f32)).reshape(bsz, t, N_BRANCH, D_MODEL)
    merged = gates[:, :, 0] * branch_r + gates[:, :, 1] * branch_f + gates[:, :, 2] * branch_s
    x = x + (merged.astype(x.dtype) @ w_out).astype(x.dtype)

    h = rmsnorm(x, g_mlp) @ w_up
    x = x + (jnp.square(jax.nn.relu(h)) @ w_down).astype(x.dtype)
    return x, (k_f, v_f, logf, k_s, v_s, ret_s)


def setup_inputs(seed: int = 0) -> dict:
    key = jax.random.key(seed)
    ks = jax.random.split(key, 24)
    f32 = jnp.float32
    n_pages = PAST_LEN // PAGE_SIZE
    n_used = DEC_BATCH * n_pages
    n_pool = n_used + max(1, n_used // 4)
    def nrm(k, shape, scale):
        return jax.random.normal(k, shape, f32) * scale
    kv_shape = (DEPTH, n_pool, PAGE_SIZE, N_HEADS, HEAD_DIM)
    x_prompt = nrm(ks[0], (BATCH, SEQ, D_MODEL), 1.0)
    x_sample = nrm(ks[1], (DEC_BATCH, DEC_SEQ, D_MODEL), 1.0)
    cache_fox_k = nrm(ks[2], kv_shape, 1.0)
    cache_fox_v = nrm(ks[3], kv_shape, 1.0)
    cache_fox_logf = jax.nn.log_sigmoid(FORGET_BIAS_INIT + jax.random.normal(ks[4], (DEPTH, n_pool, PAGE_SIZE, H_FOX), f32))
    cache_sb_k = nrm(ks[5], kv_shape, 1.0)
    cache_sb_v = nrm(ks[6], kv_shape, 1.0)
    state_ret = nrm(ks[7], (DEPTH, DEC_BATCH, H_RET, HEAD_DIM, HEAD_DIM), 0.1)
    page_table = jax.random.permutation(ks[8], n_pool)[:n_used].reshape(DEC_BATCH, n_pages).astype(jnp.int32)
    w_in = nrm(ks[9], (DEPTH, D_MODEL, N_IN), D_MODEL ** -0.5)
    b_gate = nrm(ks[10], (DEPTH, N_BRANCH * D_MODEL), 0.1)
    b_forget = FORGET_BIAS_INIT + nrm(ks[11], (DEPTH, H_FOX), 0.5)
    w_ret_o = nrm(ks[12], (DEPTH, MIX_WIDTH, D_MODEL), MIX_WIDTH ** -0.5)
    w_fox_o = nrm(ks[13], (DEPTH, MIX_WIDTH, D_MODEL), MIX_WIDTH ** -0.5)
    w_sb_o = nrm(ks[14], (DEPTH, MIX_WIDTH, D_MODEL), MIX_WIDTH ** -0.5)
    w_out = nrm(ks[15], (DEPTH, D_MODEL, D_MODEL), D_MODEL ** -0.5)
    w_up = nrm(ks[16], (DEPTH, D_MODEL, D_FF), D_MODEL ** -0.5)
    w_down = nrm(ks[17], (DEPTH, D_FF, D_MODEL), D_FF ** -0.5)
    g_mix = 1.0 + nrm(ks[18], (DEPTH, D_MODEL), 0.01)
    g_mlp = 1.0 + nrm(ks[19], (DEPTH, D_MODEL), 0.01)
    g_final = 1.0 + nrm(ks[20], (D_MODEL,), 0.01)
    return {'x_prompt': x_prompt, 'x_sample': x_sample,
            'cache_fox_k': cache_fox_k, 'cache_fox_v': cache_fox_v, 'cache_fox_logf': cache_fox_logf,
            'cache_sb_k': cache_sb_k, 'cache_sb_v': cache_sb_v, 'state_ret': state_ret,
            'page_table': page_table,
            'w_in': w_in, 'b_gate': b_gate, 'b_forget': b_forget,
            'w_ret_o': w_ret_o, 'w_fox_o': w_fox_o, 'w_sb_o': w_sb_o, 'w_out': w_out,
            'w_up': w_up, 'w_down': w_down, 'g_mix': g_mix, 'g_mlp': g_mlp, 'g_final': g_final}


def reference(x_prompt, x_sample, cache_fox_k, cache_fox_v, cache_fox_logf, cache_sb_k, cache_sb_v,
              state_ret, page_table, w_in, b_gate, b_forget, w_ret_o, w_fox_o, w_sb_o, w_out,
              w_up, w_down, g_mix, g_mlp, g_final):
    bp = x_prompt.shape[0]
    bs = x_sample.shape[0]
    past_len = page_table.shape[1] * PAGE_SIZE

    def gather(pool, layer):
        g = pool[layer, page_table]
        return g.reshape((bs, past_len) + g.shape[3:])

    empty_kv = jnp.zeros((bp, 0, N_HEADS, HEAD_DIM), x_prompt.dtype)
    empty_lf = jnp.zeros((bp, 0, H_FOX), jnp.float32)
    zero_state = jnp.zeros((bp, H_RET, HEAD_DIM, HEAD_DIM), jnp.float32)

    hp = x_prompt
    hs = x_sample
    new_p = []
    new_s = []
    for layer in range(DEPTH):
        params = (w_in[layer], b_gate[layer], b_forget[layer], w_ret_o[layer], w_fox_o[layer],
                  w_sb_o[layer], w_out[layer], w_up[layer], w_down[layer], g_mix[layer], g_mlp[layer])
        hp, st_p = decoder_layer(hp, 0, empty_kv, empty_kv, empty_lf, empty_kv, empty_kv,
                                 zero_state, *params)
        hs, st_s = decoder_layer(hs, past_len, gather(cache_fox_k, layer), gather(cache_fox_v, layer),
                                 gather(cache_fox_logf, layer), gather(cache_sb_k, layer),
                                 gather(cache_sb_v, layer), state_ret[layer], *params)
        new_p.append(st_p)
        new_s.append(st_s)

    def stack(lst, i):
        return jnp.stack([st[i] for st in lst], axis=0)

    y_prompt = rmsnorm(hp, g_final)
    y_sample = rmsnorm(hs, g_final)
    return (y_prompt, y_sample,
            stack(new_p, 0), stack(new_p, 1), stack(new_p, 2), stack(new_p, 3), stack(new_p, 4), stack(new_p, 5),
            stack(new_s, 0), stack(new_s, 1), stack(new_s, 2), stack(new_s, 3), stack(new_s, 4), stack(new_s, 5))
```

```python
import functools
import math

import jax
import jax.numpy as jnp
from jax import lax
from jax.experimental import pallas as pl
from jax.experimental.pallas import tpu as pltpu

HEAD_DIM = 128
ROPE_BASE = 10000.0
EPS = 1e-6
N_BRANCH = 3
SAMPLE_ROWS = 8
SEQ_PER_GROUP = 16
PAGES_PER_STEP = 4
CUMSUM_SUB = 256
VMEM_LIMIT = 56 * 1024 * 1024

F32 = jnp.float32
BF16 = jnp.bfloat16


def _cparams(sem):
    return pltpu.CompilerParams(dimension_semantics=sem, vmem_limit_bytes=VMEM_LIMIT)


def _tile(n, pref, align):
    t = min(pref, n)
    t -= t % align
    while t > align and n % t:
        t -= align
    assert t >= align and n % t == 0, (n, pref, align)
    return t


def _dot(a, b):
    return jnp.dot(a, b, preferred_element_type=F32)


def _dot_nt(a, b):
    return lax.dot_general(a, b, (((1,), (1,)), ((), ())), preferred_element_type=F32)


def _split_bf16(x, terms):
    out = []
    r = x
    for _ in range(terms):
        p = r.astype(BF16)
        out.append(p)
        r = r - p.astype(F32)
    return out


def _div_pow2(x, n):
    assert n & (n - 1) == 0
    return x >> (n.bit_length() - 1)


def _sigmoid(x):
    return 1.0 / (1.0 + jnp.exp(-x))


def _log_sigmoid(x):
    return jnp.minimum(x, 0.0) - jnp.log(1.0 + jnp.exp(-jnp.abs(x)))


def _rmsnorm_kernel(x_ref, g_ref, o_ref):
    x = x_ref[...]
    y = x * lax.rsqrt(jnp.mean(x * x, axis=-1, keepdims=True) + EPS)
    o_ref[...] = (y * g_ref[...]).astype(o_ref.dtype)


def _rmsnorm(x, g, out_dtype):
    m, d = x.shape
    tm = _tile(m, 512, 8)
    return pl.pallas_call(
        _rmsnorm_kernel,
        grid=(m // tm,),
        in_specs=[pl.BlockSpec((tm, d), lambda i: (i, 0)), pl.BlockSpec((1, d), lambda i: (0, 0))],
        out_specs=pl.BlockSpec((tm, d), lambda i: (i, 0)),
        out_shape=jax.ShapeDtypeStruct((m, d), out_dtype),
        compiler_params=_cparams(("parallel",)),
        name="rmsnorm",
    )(x, g.reshape(1, d))


def _matmul_kernel(a_ref, w_ref, o_ref):
    o_ref[...] = _dot(a_ref[...], w_ref[...]).astype(o_ref.dtype)


def _matmul(a, w, out_dtype):
    m, k = a.shape
    n = w.shape[1]
    tm, tn = _tile(m, 1024, 16), _tile(n, 1024, 128)
    return pl.pallas_call(
        _matmul_kernel,
        grid=(m // tm, n // tn),
        in_specs=[pl.BlockSpec((tm, k), lambda i, j: (i, 0)), pl.BlockSpec((k, tn), lambda i, j: (0, j))],
        out_specs=pl.BlockSpec((tm, tn), lambda i, j: (i, j)),
        out_shape=jax.ShapeDtypeStruct((m, n), out_dtype),
        compiler_params=_cparams(("parallel", "parallel")),
        name="in_proj",
    )(a, w)


def _forget_kernel(a_ref, w_ref, b_ref, o_ref):
    o_ref[...] = _log_sigmoid(_dot(a_ref[...], w_ref[...]) + b_ref[...])


def _forget_proj(a, w, b):
    m, k = a.shape
    n = w.shape[1]
    tm = _tile(m, 1024, 16)
    return pl.pallas_call(
        _forget_kernel,
        grid=(m // tm,),
        in_specs=[pl.BlockSpec((tm, k), lambda i: (i, 0)), pl.BlockSpec((k, n), lambda i: (0, 0)),
                  pl.BlockSpec((1, n), lambda i: (0, 0))],
        out_specs=pl.BlockSpec((tm, n), lambda i: (i, 0)),
        out_shape=jax.ShapeDtypeStruct((m, n), F32),
        compiler_params=_cparams(("parallel",)),
        name="forget_proj",
    )(a, w, b)


def _merge_kernel(or_ref, of_ref, os_ref, wr_ref, wf_ref, ws_ref, gr_ref, gf_ref, gs_ref,
                  br_ref, bf_ref, bs_ref, o_ref):
    acc = _sigmoid(gr_ref[...] + br_ref[...]) * _dot(or_ref[...], wr_ref[...])
    acc += _sigmoid(gf_ref[...] + bf_ref[...]) * _dot(of_ref[...], wf_ref[...])
    acc += _sigmoid(gs_ref[...] + bs_ref[...]) * _dot(os_ref[...], ws_ref[...])
    o_ref[...] = acc.astype(o_ref.dtype)


def _merge(o_r, o_f, o_s, w_r, w_f, w_s, proj, gate_col0, b_gate):
    m, kk = o_r.shape
    d = w_r.shape[1]
    tm, tn = _tile(m, 512, 16), _tile(d, 1024, 128)
    gblk = gate_col0 // tn
    nblk = d // tn
    o_spec = pl.BlockSpec((tm, kk), lambda i, j: (i, 0))
    w_spec = pl.BlockSpec((kk, tn), lambda i, j: (0, j))

    def g_spec(r):
        return pl.BlockSpec((tm, tn), lambda i, j: (i, gblk + r * nblk + j))

    def b_spec(r):
        return pl.BlockSpec((1, tn), lambda i, j: (0, r * nblk + j))

    b2 = b_gate.reshape(1, N_BRANCH * d)
    return pl.pallas_call(
        _merge_kernel,
        grid=(m // tm, d // tn),
        in_specs=[o_spec, o_spec, o_spec, w_spec, w_spec, w_spec, g_spec(0), g_spec(1), g_spec(2),
                  b_spec(0), b_spec(1), b_spec(2)],
        out_specs=pl.BlockSpec((tm, tn), lambda i, j: (i, j)),
        out_shape=jax.ShapeDtypeStruct((m, d), BF16),
        compiler_params=_cparams(("parallel", "parallel")),
        name="merge",
    )(o_r, o_f, o_s, w_r, w_f, w_s, proj, proj, proj, b2, b2, b2)


def _matmul_resid_kernel(a_ref, w_ref, r_ref, o_ref):
    o_ref[...] = r_ref[...] + _dot(a_ref[...], w_ref[...])


def _matmul_resid(a, w, resid):
    m, k = a.shape
    n = w.shape[1]
    tm, tn = _tile(m, 1024, 16), _tile(n, 1024, 128)
    return pl.pallas_call(
        _matmul_resid_kernel,
        grid=(m // tm, n // tn),
        in_specs=[pl.BlockSpec((tm, k), lambda i, j: (i, 0)), pl.BlockSpec((k, tn), lambda i, j: (0, j)),
                  pl.BlockSpec((tm, tn), lambda i, j: (i, j))],
        out_specs=pl.BlockSpec((tm, tn), lambda i, j: (i, j)),
        out_shape=jax.ShapeDtypeStruct((m, n), F32),
        compiler_params=_cparams(("parallel", "parallel")),
        name="out_proj",
    )(a, w, resid)


def _up_kernel(a_ref, w_ref, o_ref):
    h = jnp.maximum(_dot(a_ref[...], w_ref[...]), 0.0)
    o_ref[...] = (h * h).astype(o_ref.dtype)


def _mlp_up(a, w):
    m, k = a.shape
    n = w.shape[1]
    tm, tn = _tile(m, 1024, 16), _tile(n, 1024, 128)
    return pl.pallas_call(
        _up_kernel,
        grid=(m // tm, n // tn),
        in_specs=[pl.BlockSpec((tm, k), lambda i, j: (i, 0)), pl.BlockSpec((k, tn), lambda i, j: (0, j))],
        out_specs=pl.BlockSpec((tm, tn), lambda i, j: (i, j)),
        out_shape=jax.ShapeDtypeStruct((m, n), BF16),
        compiler_params=_cparams(("parallel", "parallel")),
        name="mlp_up",
    )(a, w)


def _down_kernel(a_ref, w_ref, r_ref, o_ref, acc_ref):
    kk = pl.program_id(2)

    @pl.when(kk == 0)
    def _():
        acc_ref[...] = r_ref[...]

    acc_ref[...] += _dot(a_ref[...], w_ref[...])

    @pl.when(kk == pl.num_programs(2) - 1)
    def _():
        o_ref[...] = acc_ref[...]


def _mlp_down(a, w, resid):
    m, k = a.shape
    n = w.shape[1]
    tm, tn, tk = _tile(m, 1024, 16), _tile(n, 1024, 128), _tile(k, 2048, 128)
    return pl.pallas_call(
        _down_kernel,
        grid=(m // tm, n // tn, k // tk),
        in_specs=[pl.BlockSpec((tm, tk), lambda i, j, kk: (i, kk)), pl.BlockSpec((tk, tn), lambda i, j, kk: (kk, j)),
                  pl.BlockSpec((tm, tn), lambda i, j, kk: (i, j))],
        out_specs=pl.BlockSpec((tm, tn), lambda i, j, kk: (i, j)),
        out_shape=jax.ShapeDtypeStruct((m, n), F32),
        scratch_shapes=[pltpu.VMEM((tm, tn), F32)],
        compiler_params=_cparams(("parallel", "parallel", "arbitrary")),
        name="mlp_down",
    )(a, w, resid)


def _cumsum_kernel(x_ref, o_ref, carry_ref):
    c = pl.program_id(1)

    @pl.when(c == 0)
    def _():
        carry_ref[...] = jnp.zeros_like(carry_ref)

    n = x_ref.shape[1]
    row = lax.broadcasted_iota(jnp.int32, (n, n), 0)
    col = lax.broadcasted_iota(jnp.int32, (n, n), 1)
    tri = jnp.where(col <= row, 1.0, 0.0).astype(BF16)
    acc = carry_ref[...]
    for piece in _split_bf16(x_ref[0], 3):
        acc = acc + _dot(tri, piece)
    o_ref[0] = acc
    carry_ref[...] = jnp.broadcast_to(acc[n - 1:n, :], carry_ref.shape)


def _cumsum_rows(x):
    r, t, lanes = x.shape
    n = 128
    return pl.pallas_call(
        _cumsum_kernel,
        grid=(r, t // n),
        in_specs=[pl.BlockSpec((1, n, lanes), lambda i, c: (i, c, 0))],
        out_specs=pl.BlockSpec((1, n, lanes), lambda i, c: (i, c, 0)),
        out_shape=jax.ShapeDtypeStruct((r, t, lanes), F32),
        scratch_shapes=[pltpu.VMEM((n, lanes), F32)],
        compiler_params=_cparams(("parallel", "arbitrary")),
        name="logf_cumsum",
    )(x)


def _gather_logf_kernel(pt_ref, *refs):
    o_ref = refs[-1]
    for p, x_ref in enumerate(refs[:-1]):
        o_ref[0, p] = x_ref[0, 0]


def _gather_logf(cache_logf, page_table, layer):
    bs, n_pages = page_table.shape
    _, _, page, h = cache_logf.shape
    in_specs = [pl.BlockSpec((1, 1, page, h), functools.partial(lambda b, pt, p: (layer, pt[b, p], 0, 0), p=p))
                for p in range(n_pages)]
    return pl.pallas_call(
        _gather_logf_kernel,
        grid_spec=pltpu.PrefetchScalarGridSpec(
            num_scalar_prefetch=1, grid=(bs,), in_specs=in_specs,
            out_specs=pl.BlockSpec((1, n_pages, page, h), lambda b, pt: (b, 0, 0, 0))),
        out_shape=jax.ShapeDtypeStruct((bs, n_pages, page, h), F32),
        compiler_params=_cparams(("arbitrary",)),
        name="gather_logf",
    )(page_table, *([cache_logf] * n_pages))


def _rope(x, cosf, sinf):
    return x * cosf + pltpu.roll(x, HEAD_DIM // 2, 1) * sinf


def _log_gamma(h):
    return math.log(1.0 - 2.0 ** (-5.0 - h))


def _head_norm_gate(o, g):
    on = o * lax.rsqrt(jnp.mean(o * o, axis=-1, keepdims=True) + EPS)
    return on * (g * _sigmoid(g))


def _ret_prompt_kernel(q_ref, k_ref, v_ref, g_ref, cos_ref, sin_ref, o_ref, s_ref, *, n_heads):
    c = q_ref.shape[0]

    @pl.when(pl.program_id(1) == 0)
    def _():
        s_ref[...] = jnp.zeros_like(s_ref)

    cosf, sinf = cos_ref[...], sin_ref[...]
    ii = lax.broadcasted_iota(jnp.int32, (c, c), 0).astype(F32)
    jj = lax.broadcasted_iota(jnp.int32, (c, c), 1).astype(F32)
    diff = ii - jj
    scale = HEAD_DIM ** -0.5
    for h in range(n_heads):
        lg = _log_gamma(h)
        hs = slice(h * HEAD_DIM, (h + 1) * HEAD_DIM)
        inner = jnp.where(diff >= 0, jnp.exp(diff * lg), 0.0)
        cross = jnp.exp((ii + 1.0) * lg)
        kdec = jnp.exp((c - 1.0 - ii) * lg)
        qr = _rope(q_ref[:, hs], cosf, sinf)
        kr = _rope(k_ref[:, hs], cosf, sinf) * scale
        qb, kb, vb = qr.astype(BF16), kr.astype(BF16), v_ref[:, hs].astype(BF16)
        a = _dot_nt(qb, kb) * inner
        s = s_ref[0, h]
        o = _dot(a.astype(BF16), vb) + _dot(qb, s.astype(BF16)) * cross
        s_ref[0, h] = s * math.exp(c * lg) + _dot((kr * kdec).T.astype(BF16), vb)
        o_ref[:, hs] = _head_norm_gate(o, g_ref[:, hs]).astype(o_ref.dtype)


def _ret_prompt(proj, cosf, sinf, batch, seq, n_heads, col0):
    mix = n_heads * HEAD_DIM
    c = 128
    nc = seq // c
    cb = col0 // mix

    def seg(r):
        return pl.BlockSpec((c, mix), lambda b, i: (b * nc + i, cb + r))

    tab = pl.BlockSpec((c, HEAD_DIM), lambda b, i: (i, 0))
    return pl.pallas_call(
        functools.partial(_ret_prompt_kernel, n_heads=n_heads),
        grid=(batch, nc),
        in_specs=[seg(0), seg(1), seg(2), seg(3), tab, tab],
        out_specs=[pl.BlockSpec((c, mix), lambda b, i: (b * nc + i, 0)),
                   pl.BlockSpec((1, n_heads, HEAD_DIM, HEAD_DIM), lambda b, i: (b, 0, 0, 0))],
        out_shape=[jax.ShapeDtypeStruct((batch * seq, mix), BF16),
                   jax.ShapeDtypeStruct((batch, n_heads, HEAD_DIM, HEAD_DIM), F32)],
        compiler_params=_cparams(("parallel", "arbitrary")),
        name="retention_prompt",
    )(proj, proj, proj, proj, cosf, sinf)


def _ret_sample_kernel(q_ref, k_ref, v_ref, g_ref, cos_ref, sin_ref, s0_ref, o_ref, s_ref, *, t_real):
    n = q_ref.shape[0]
    hf = jnp.full((n, n), pl.program_id(1), jnp.int32).astype(F32)
    lg = jnp.log(1.0 - jnp.exp2(-5.0 - hf))
    ri = lax.broadcasted_iota(jnp.int32, (n, n), 0)
    ci = lax.broadcasted_iota(jnp.int32, (n, n), 1)
    rseq, cseq = _div_pow2(ri, SAMPLE_ROWS), _div_pow2(ci, SAMPLE_ROWS)
    il, jl = ri & (SAMPLE_ROWS - 1), ci & (SAMPLE_ROWS - 1)
    ilf = il.astype(F32)
    causal = jnp.logical_and(rseq == cseq, jnp.logical_and(jl <= il, il < t_real))
    inner = jnp.where(causal, jnp.exp((il - jl).astype(F32) * lg), 0.0)
    cross = jnp.exp((ilf + 1.0) * lg)
    kdec = jnp.where(il < t_real, jnp.exp((t_real - 1.0 - ilf) * lg), 0.0)
    cdec = jnp.exp(float(t_real) * lg)
    cosf, sinf = cos_ref[...], sin_ref[...]
    qr = _rope(q_ref[...], cosf, sinf)
    kr = _rope(k_ref[...], cosf, sinf) * (HEAD_DIM ** -0.5)
    qb, kb, vb = qr.astype(BF16), kr.astype(BF16), v_ref[...].astype(BF16)
    o = _dot((_dot_nt(qb, kb) * inner).astype(BF16), vb)
    kdt = (kr * kdec).T
    for sq in range(n // SAMPLE_ROWS):
        s = s0_ref[sq, 0]
        o = jnp.where(rseq == sq, o + _dot(qb, s.astype(BF16)) * cross, o)
        s_ref[sq, 0] = s * cdec + _dot(jnp.where(cseq == sq, kdt, 0.0).astype(BF16), vb)
    o_ref[...] = _head_norm_gate(o, g_ref[...]).astype(o_ref.dtype)


def _ret_sample(proj, cosf, sinf, s0, row0, t_real, col0):
    bs, n_heads = s0.shape[:2]
    mix = n_heads * HEAD_DIM
    n = SEQ_PER_GROUP * SAMPLE_ROWS
    rb = row0 // n
    cb = col0 // HEAD_DIM

    def seg(r):
        return pl.BlockSpec((n, HEAD_DIM), lambda gidx, h: (rb + gidx, cb + r * n_heads + h))

    tab = pl.BlockSpec((n, HEAD_DIM), lambda gidx, h: (0, 0))
    st = pl.BlockSpec((SEQ_PER_GROUP, 1, HEAD_DIM, HEAD_DIM), lambda gidx, h: (gidx, h, 0, 0))
    return pl.pallas_call(
        functools.partial(_ret_sample_kernel, t_real=t_real),
        grid=(bs // SEQ_PER_GROUP, n_heads),
        in_specs=[seg(0), seg(1), seg(2), seg(3), tab, tab, st],
        out_specs=[pl.BlockSpec((n, HEAD_DIM), lambda gidx, h: (gidx, h)), st],
        out_shape=[jax.ShapeDtypeStruct((bs * SAMPLE_ROWS, mix), BF16),
                   jax.ShapeDtypeStruct(s0.shape, F32)],
        compiler_params=_cparams(("parallel", "arbitrary")),
        name="retention_sample",
    )(proj, proj, proj, proj, cosf, sinf, s0)


def _pair_tables(nq, descending):
    qi, ki = [], []
    for q in range(nq):
        ks = range(q, -1, -1) if descending else range(q + 1)
        for k in ks:
            qi.append(q)
            ki.append(k)
    return jnp.asarray(qi, jnp.int32), jnp.asarray(ki, jnp.int32)


def _fox_prompt_kernel(qi_ref, ki_ref, q_ref, k_ref, v_ref, nc_ref, o_ref, m_ref, l_ref, acc_ref, *, n_heads):
    p = pl.program_id(1)
    qi, ki = qi_ref[p], ki_ref[p]
    tq, tk = q_ref.shape[0], k_ref.shape[0]

    @pl.when(ki == 0)
    def _():
        m_ref[...] = jnp.full_like(m_ref, -jnp.inf)
        l_ref[...] = jnp.zeros_like(l_ref)
        acc_ref[...] = jnp.zeros_like(acc_ref)

    rows = qi * tq + lax.broadcasted_iota(jnp.int32, (tq, tk), 0)
    cols = ki * tk + lax.broadcasted_iota(jnp.int32, (tq, tk), 1)
    valid = cols <= rows
    scale = HEAD_DIM ** -0.5
    for h in range(n_heads):
        hs = slice(h * HEAD_DIM, (h + 1) * HEAD_DIM)
        qb, kb, vb = q_ref[:, hs].astype(BF16), k_ref[:, hs].astype(BF16), v_ref[:, hs].astype(BF16)
        s = _dot_nt(qb, kb) * scale + nc_ref[0, h]
        s = jnp.where(valid, s, -jnp.inf)
        m_prev = m_ref[h]
        m_new = jnp.maximum(m_prev, jnp.max(s, axis=-1, keepdims=True))
        alpha = jnp.exp(m_prev - m_new)
        pr = jnp.exp(s - m_new)
        l_ref[h] = alpha * l_ref[h] + jnp.sum(pr, axis=-1, keepdims=True)
        acc_ref[h] = alpha * acc_ref[h] + _dot(pr.astype(BF16), vb)
        m_ref[h] = m_new

    @pl.when(ki == qi)
    def _():
        for h in range(n_heads):
            o_ref[:, h * HEAD_DIM:(h + 1) * HEAD_DIM] = (acc_ref[h] / l_ref[h]).astype(o_ref.dtype)


def _sb_prompt_kernel(qi_ref, ki_ref, q_ref, k_ref, v_ref, o_ref, carry_ref, acc_ref, *, n_heads):
    p = pl.program_id(1)
    qi, ki = qi_ref[p], ki_ref[p]
    tq, tk = q_ref.shape[0], k_ref.shape[0]
    sub = min(CUMSUM_SUB, tk)

    @pl.when(ki == qi)
    def _():
        carry_ref[...] = jnp.zeros_like(carry_ref)
        acc_ref[...] = jnp.zeros_like(acc_ref)

    ur = lax.broadcasted_iota(jnp.int32, (sub, sub), 0)
    uc = lax.broadcasted_iota(jnp.int32, (sub, sub), 1)
    upper = jnp.where(ur >= uc, 1.0, 0.0).astype(BF16)
    rows = qi * tq + lax.broadcasted_iota(jnp.int32, (tq, sub), 0)
    col0 = ki * tk + lax.broadcasted_iota(jnp.int32, (tq, sub), 1)
    scale = HEAD_DIM ** -0.5
    for h in range(n_heads):
        hs = slice(h * HEAD_DIM, (h + 1) * HEAD_DIM)
        qb = q_ref[:, hs].astype(BF16)
        carry = carry_ref[h]
        acc = acc_ref[h]
        for sb in range(tk // sub - 1, -1, -1):
            ks = slice(sb * sub, (sb + 1) * sub)
            kb, vb = k_ref[ks, hs].astype(BF16), v_ref[ks, hs].astype(BF16)
            valid = col0 + sb * sub < rows
            z = _dot_nt(qb, kb) * scale
            lneg = -(jnp.maximum(z, 0.0) + jnp.log(1.0 + jnp.exp(-jnp.abs(z))))
            lneg = jnp.where(valid, lneg, 0.0)
            hi, lo = _split_bf16(lneg, 2)
            incl = _dot(hi, upper) + _dot(lo, upper)
            a = jnp.exp(jnp.where(valid, z + incl + carry, -1e30))
            acc = acc + _dot(a.astype(BF16), vb)
            carry = carry + incl[:, 0:1]
        carry_ref[h] = carry
        acc_ref[h] = acc

    @pl.when(ki == 0)
    def _():
        for h in range(n_heads):
            o_ref[:, h * HEAD_DIM:(h + 1) * HEAD_DIM] = acc_ref[h].astype(o_ref.dtype)


def _prompt_attention(kind, proj, batch, seq, n_heads, col0, neg_cum=None):
    mix = n_heads * HEAD_DIM
    t = _tile(seq, 512, 128)
    nq = seq // t
    cb = col0 // mix
    qi, ki = _pair_tables(nq, descending=(kind == "sb"))
    q_spec = pl.BlockSpec((t, mix), lambda b, p, qt, kt: (b * nq + qt[p], cb))
    k_spec = pl.BlockSpec((t, mix), lambda b, p, qt, kt: (b * nq + kt[p], cb + 1))
    v_spec = pl.BlockSpec((t, mix), lambda b, p, qt, kt: (b * nq + kt[p], cb + 2))
    o_spec = pl.BlockSpec((t, mix), lambda b, p, qt, kt: (b * nq + qt[p], 0))
    col_state = pltpu.VMEM((n_heads, t, 1), F32)
    acc_state = pltpu.VMEM((n_heads, t, HEAD_DIM), F32)
    if kind == "fox":
        body = functools.partial(_fox_prompt_kernel, n_heads=n_heads)
        in_specs = [q_spec, k_spec, v_spec,
                    pl.BlockSpec((1, n_heads, 1, t), lambda b, p, qt, kt: (b, 0, 0, kt[p]))]
        args = (proj, proj, proj, neg_cum)
        scratch = [col_state, col_state, acc_state]
    else:
        body = functools.partial(_sb_prompt_kernel, n_heads=n_heads)
        in_specs = [q_spec, k_spec, v_spec]
        args = (proj, proj, proj)
        scratch = [col_state, acc_state]
    return pl.pallas_call(
        body,
        grid_spec=pltpu.PrefetchScalarGridSpec(
            num_scalar_prefetch=2, grid=(batch, int(qi.shape[0])), in_specs=in_specs, out_specs=o_spec,
            scratch_shapes=scratch),
        out_shape=jax.ShapeDtypeStruct((batch * seq, mix), BF16),
        compiler_params=_cparams(("parallel", "arbitrary")),
        name=kind + "_prompt",
    )(qi, ki, *args)


def _diag_rows(acc_ref, o_ref, t_real, n_heads):
    mix = n_heads * HEAD_DIM
    hrow = lax.broadcasted_iota(jnp.int32, (n_heads, mix), 0)
    hcol = _div_pow2(lax.broadcasted_iota(jnp.int32, (n_heads, mix), 1), HEAD_DIM)
    o_ref[0] = jnp.zeros(o_ref.shape[1:], o_ref.dtype)
    for i in range(t_real):
        blk = acc_ref[i * n_heads:(i + 1) * n_heads, :]
        o_ref[0, i:i + 1, :] = jnp.sum(jnp.where(hrow == hcol, blk, 0.0), axis=0, keepdims=True)


def _new_keys(kn_ref, vn_ref, pad_ref):
    pad_ref[0, 0:SAMPLE_ROWS, :] = kn_ref[...]
    pad_ref[1, 0:SAMPLE_ROWS, :] = vn_ref[...]
    return pad_ref[0].astype(BF16), pad_ref[1].astype(BF16)


def _fox_sample_kernel(pt_ref, q_ref, kn_ref, vn_ref, nc_ref, *refs, n_heads, t_real, n_steps):
    pages = refs[:2 * PAGES_PER_STEP]
    o_ref, m_ref, l_ref, acc_ref, pad_ref = refs[2 * PAGES_PER_STEP:]
    g = pl.program_id(1)
    rows = t_real * n_heads
    page = pad_ref.shape[1]

    @pl.when(jnp.logical_and(pl.program_id(0) == 0, g == 0))
    def _():
        pad_ref[...] = jnp.zeros_like(pad_ref)

    @pl.when(g == 0)
    def _():
        m_ref[...] = jnp.full_like(m_ref, -jnp.inf)
        l_ref[...] = jnp.zeros_like(l_ref)
        acc_ref[...] = jnp.zeros_like(acc_ref)

    qb = q_ref[0]
    scale = HEAD_DIM ** -0.5

    def block(kb, vb, bias, valid):
        s = _dot_nt(qb, kb) * scale + jnp.concatenate([bias] * t_real, axis=0)
        if valid is not None:
            s = jnp.where(valid, s, -jnp.inf)
        m_prev = m_ref[...]
        m_new = jnp.maximum(m_prev, jnp.max(s, axis=-1, keepdims=True))
        alpha = jnp.exp(m_prev - m_new)
        pr = jnp.exp(s - m_new)
        l_ref[...] = alpha * l_ref[...] + jnp.sum(pr, axis=-1, keepdims=True)
        acc_ref[...] = alpha * acc_ref[...] + _dot(pr.astype(BF16), vb)
        m_ref[...] = m_new

    @pl.when(g < n_steps)
    def _():
        for pp in range(PAGES_PER_STEP):
            kb = pages[pp][0, 0].astype(BF16)
            vb = pages[PAGES_PER_STEP + pp][0, 0].astype(BF16)
            block(kb, vb, nc_ref[0, g * PAGES_PER_STEP + pp], None)

    @pl.when(g == n_steps)
    def _():
        kb, vb = _new_keys(kn_ref, vn_ref, pad_ref)
        qidx = _div_pow2(lax.broadcasted_iota(jnp.int32, (rows, page), 0), n_heads)
        kidx = lax.broadcasted_iota(jnp.int32, (rows, page), 1)
        block(kb, vb, nc_ref[0, n_steps * PAGES_PER_STEP], kidx <= qidx)
        acc_ref[...] = acc_ref[...] / l_ref[...]
        _diag_rows(acc_ref, o_ref, t_real, n_heads)


def _sb_sample_kernel(pt_ref, q_ref, kn_ref, vn_ref, *refs, n_heads, t_real, n_steps):
    pages = refs[:2 * PAGES_PER_STEP]
    o_ref, carry_ref, acc_ref, pad_ref = refs[2 * PAGES_PER_STEP:]
    g = pl.program_id(1)
    rows = t_real * n_heads
    page = pad_ref.shape[1]

    @pl.when(jnp.logical_and(pl.program_id(0) == 0, g == 0))
    def _():
        pad_ref[...] = jnp.zeros_like(pad_ref)

    ur = lax.broadcasted_iota(jnp.int32, (page, page), 0)
    uc = lax.broadcasted_iota(jnp.int32, (page, page), 1)
    upper = jnp.where(ur >= uc, 1.0, 0.0).astype(BF16)
    qb = q_ref[0]
    scale = HEAD_DIM ** -0.5

    def block(kb, vb, valid):
        z = _dot_nt(qb, kb) * scale
        lneg = -(jnp.maximum(z, 0.0) + jnp.log(1.0 + jnp.exp(-jnp.abs(z))))
        if valid is not None:
            lneg = jnp.where(valid, lneg, 0.0)
        hi, lo = _split_bf16(lneg, 2)
        incl = _dot(hi, upper) + _dot(lo, upper)
        ex = z + incl + carry_ref[...]
        if valid is not None:
            ex = jnp.where(valid, ex, -1e30)
        acc_ref[...] += _dot(jnp.exp(ex).astype(BF16), vb)
        carry_ref[...] += incl[:, 0:1]

    @pl.when(g == 0)
    def _():
        carry_ref[...] = jnp.zeros_like(carry_ref)
        acc_ref[...] = jnp.zeros_like(acc_ref)
        kb, vb = _new_keys(kn_ref, vn_ref, pad_ref)
        qidx = _div_pow2(lax.broadcasted_iota(jnp.int32, (rows, page), 0), n_heads)
        kidx = lax.broadcasted_iota(jnp.int32, (rows, page), 1)
        block(kb, vb, kidx < qidx)

    @pl.when(g > 0)
    def _():
        for pp in range(PAGES_PER_STEP - 1, -1, -1):
            block(pages[pp][0, 0].astype(BF16), pages[PAGES_PER_STEP + pp][0, 0].astype(BF16), None)

    @pl.when(g == n_steps)
    def _():
        _diag_rows(acc_ref, o_ref, t_real, n_heads)


def _sample_attention(kind, qbd, proj, cache_k, cache_v, page_table, layer, row0, col0, t_real, neg_cum=None):
    bs, n_pages = page_table.shape
    _, n_pool, page, n_heads, _ = cache_k.shape
    mix = n_heads * HEAD_DIM
    rows = t_real * n_heads
    n_steps = n_pages // PAGES_PER_STEP
    assert n_pages % PAGES_PER_STEP == 0
    ck = cache_k.reshape(cache_k.shape[0], n_pool, page, mix)
    cv = cache_v.reshape(cache_v.shape[0], n_pool, page, mix)
    rb = row0 // SAMPLE_ROWS
    cb = col0 // mix
    descending = kind == "sb"

    def page_spec(pp):
        if descending:
            def imap(b, g, pt):
                gg = jnp.maximum(g - 1, 0)
                return (layer, pt[b, (n_steps - 1 - gg) * PAGES_PER_STEP + pp], 0, 0)
        else:
            def imap(b, g, pt):
                gg = jnp.minimum(g, n_steps - 1)
                return (layer, pt[b, gg * PAGES_PER_STEP + pp], 0, 0)
        return pl.BlockSpec((1, 1, page, mix), imap)

    q_spec = pl.BlockSpec((1, rows, mix), lambda b, g, pt: (b, 0, 0))
    kn_spec = pl.BlockSpec((SAMPLE_ROWS, mix), lambda b, g, pt: (rb + b, cb + 1))
    vn_spec = pl.BlockSpec((SAMPLE_ROWS, mix), lambda b, g, pt: (rb + b, cb + 2))
    page_specs = [page_spec(pp) for pp in range(PAGES_PER_STEP)]
    o_spec = pl.BlockSpec((1, SAMPLE_ROWS, mix), lambda b, g, pt: (b, 0, 0))
    col_state = pltpu.VMEM((rows, 1), F32)
    acc_state = pltpu.VMEM((rows, mix), F32)
    pad_state = pltpu.VMEM((2, page, mix), F32)
    common = dict(n_heads=n_heads, t_real=t_real, n_steps=n_steps)
    if kind == "fox":
        body = functools.partial(_fox_sample_kernel, **common)
        in_specs = [q_spec, kn_spec, vn_spec,
                    pl.BlockSpec((1, n_pages + 1, n_heads, page), lambda b, g, pt: (b, 0, 0, 0))]
        args = (qbd, proj, proj, neg_cum)
        scratch = [col_state, col_state, acc_state, pad_state]
    else:
        body = functools.partial(_sb_sample_kernel, **common)
        in_specs = [q_spec, kn_spec, vn_spec]
        args = (qbd, proj, proj)
        scratch = [col_state, acc_state, pad_state]
    return pl.pallas_call(
        body,
        grid_spec=pltpu.PrefetchScalarGridSpec(
            num_scalar_prefetch=1, grid=(bs, n_steps + 1), in_specs=in_specs + page_specs + page_specs,
            out_specs=o_spec, scratch_shapes=scratch),
        out_shape=jax.ShapeDtypeStruct((bs, SAMPLE_ROWS, mix), F32),
        compiler_params=_cparams(("arbitrary", "arbitrary")),
        name=kind + "_sample",
    )(page_table, *args, *([ck] * PAGES_PER_STEP), *([cv] * PAGES_PER_STEP))


def _rope_tables(pos):
    half = HEAD_DIM // 2
    inv = ROPE_BASE ** (-jnp.arange(half, dtype=F32) / half)
    ang = pos.astype(F32)[:, None] * inv[None, :]
    cos, sin = jnp.cos(ang), jnp.sin(ang)
    return jnp.concatenate([cos, cos], axis=-1), jnp.concatenate([-sin, sin], axis=-1)


def _block_diag_queries(q, t_real):
    bs, _, n_heads, _ = q.shape
    eye = jnp.eye(n_heads, dtype=q.dtype)
    qbd = q[:, :t_real, :, None, :] * eye[None, None, :, :, None]
    return qbd.reshape(bs, t_real * n_heads, n_heads * HEAD_DIM).astype(BF16)


def kernel(x_prompt, x_sample, cache_fox_k, cache_fox_v, cache_fox_logf, cache_sb_k, cache_sb_v, state_ret,
           page_table, w_in, b_gate, b_forget, w_ret_o, w_fox_o, w_sb_o, w_out, w_up, w_down, g_mix, g_mlp,
           g_final):
    bp, seq, d = x_prompt.shape
    bs, t_real, _ = x_sample.shape
    depth, _, page, n_heads, _ = cache_fox_k.shape
    n_pages = page_table.shape[1]
    past = n_pages * page
    mix = n_heads * HEAD_DIM
    m_p = bp * seq
    m_s = bs * SAMPLE_ROWS
    assert t_real <= SAMPLE_ROWS and bs % SEQ_PER_GROUP == 0 and page == 128 and seq % 128 == 0
    c_ret, c_fox, c_sb, c_gate = 0, 4 * mix, 7 * mix, 10 * mix
    f0 = 7 * mix

    xs = jnp.pad(x_sample, ((0, 0), (0, SAMPLE_ROWS - t_real), (0, 0)))
    x = jnp.concatenate([x_prompt.reshape(m_p, d), xs.reshape(m_s, d)], axis=0)

    cos_p, sin_p = _rope_tables(jnp.arange(seq))
    cos_s, sin_s = _rope_tables(past + jnp.arange(SAMPLE_ROWS))
    cos_s, sin_s = jnp.tile(cos_s, (SEQ_PER_GROUP, 1)), jnp.tile(sin_s, (SEQ_PER_GROUP, 1))
    lane_pad = 128 - n_heads

    outs_p = [[] for _ in range(6)]
    outs_s = [[] for _ in range(6)]
    for layer in range(depth):
        w_main = jnp.concatenate([w_in[layer, :, :f0], w_in[layer, :, f0 + n_heads:]], axis=1).astype(BF16)
        w_fg = jnp.pad(w_in[layer, :, f0:f0 + n_heads], ((0, 0), (0, lane_pad))).astype(BF16)
        b_fg = jnp.pad(b_forget[layer], (0, lane_pad)).reshape(1, 128)

        xn = _rmsnorm(x, g_mix[layer], BF16)
        proj = _matmul(xn, w_main, F32)
        logf = _forget_proj(xn, w_fg, b_fg)

        o_r_p, ret_p = _ret_prompt(proj, cos_p, sin_p, bp, seq, n_heads, c_ret)
        o_r_s, ret_s = _ret_sample(proj, cos_s, sin_s, state_ret[layer], m_p, t_real, c_ret)

        cum_p = _cumsum_rows(logf[:m_p].reshape(bp, seq, 128))
        neg_cum_p = -jnp.transpose(cum_p[:, :, :n_heads], (0, 2, 1)).reshape(bp, n_heads, 1, seq)
        o_f_p = _prompt_attention("fox", proj, bp, seq, n_heads, c_fox, neg_cum_p)

        logf_s = logf[m_p:].reshape(bs, SAMPLE_ROWS, 128)[:, :, :n_heads]
        logf_s = jnp.where(jnp.arange(SAMPLE_ROWS)[None, :, None] < t_real, logf_s, 0.0)
        logf_past = _gather_logf(cache_fox_logf, page_table, layer).reshape(bs, past, n_heads)
        total = past + page
        lf = jnp.concatenate([logf_past, logf_s, jnp.zeros((bs, page - SAMPLE_ROWS, n_heads), F32)], axis=1)
        n_grp = bs * n_heads // 128
        lf = jnp.transpose(lf.reshape(n_grp, 128 // n_heads, total, n_heads), (0, 2, 1, 3)).reshape(n_grp, total, 128)
        cum_s = _cumsum_rows(lf)
        cum_s = jnp.transpose(cum_s.reshape(n_grp, n_pages + 1, page, 128 // n_heads, n_heads), (0, 3, 1, 4, 2))
        neg_cum_s = -cum_s.reshape(bs, n_pages + 1, n_heads, page)
        q_f_s = proj[m_p:, c_fox:c_fox + mix].reshape(bs, SAMPLE_ROWS, n_heads, HEAD_DIM)
        o_f_s = _sample_attention("fox", _block_diag_queries(q_f_s, t_real), proj, cache_fox_k, cache_fox_v,
                                  page_table, layer, m_p, c_fox, t_real, neg_cum_s)

        o_s_p = _prompt_attention("sb", proj, bp, seq, n_heads, c_sb)
        q_s_s = proj[m_p:, c_sb:c_sb + mix].reshape(bs, SAMPLE_ROWS, n_heads, HEAD_DIM)
        o_s_s = _sample_attention("sb", _block_diag_queries(q_s_s, t_real), proj, cache_sb_k, cache_sb_v,
                                  page_table, layer, m_p, c_sb, t_real)

        o_r = jnp.concatenate([o_r_p, o_r_s], axis=0)
        o_f = jnp.concatenate([o_f_p, o_f_s.reshape(m_s, mix).astype(BF16)], axis=0)
        o_s = jnp.concatenate([o_s_p, o_s_s.reshape(m_s, mix).astype(BF16)], axis=0)

        merged = _merge(o_r, o_f, o_s, w_ret_o[layer].astype(BF16), w_fox_o[layer].astype(BF16),
                        w_sb_o[layer].astype(BF16), proj, c_gate, b_gate[layer])
        x = _matmul_resid(merged, w_out[layer].astype(BF16), x)
        hmid = _mlp_up(_rmsnorm(x, g_mlp[layer], BF16), w_up[layer].astype(BF16))
        x = _mlp_down(hmid, w_down[layer].astype(BF16), x)

        def seg_p(c0):
            return proj[:m_p, c0:c0 + mix].reshape(bp, seq, n_heads, HEAD_DIM)

        def seg_s(c0):
            return proj[m_p:, c0:c0 + mix].reshape(bs, SAMPLE_ROWS, n_heads, HEAD_DIM)[:, :t_real]

        for lst, val in zip(outs_p, (seg_p(c_fox + mix), seg_p(c_fox + 2 * mix),
                                     logf[:m_p, :n_heads].reshape(bp, seq, n_heads),
                                     seg_p(c_sb + mix), seg_p(c_sb + 2 * mix), ret_p)):
            lst.append(val)
        for lst, val in zip(outs_s, (seg_s(c_fox + mix), seg_s(c_fox + 2 * mix),
                                     logf[m_p:, :n_heads].reshape(bs, SAMPLE_ROWS, n_heads)[:, :t_real],
                                     seg_s(c_sb + mix), seg_s(c_sb + 2 * mix), ret_s)):
            lst.append(val)

    y = _rmsnorm(x, g_final, F32)
    y_prompt = y[:m_p].reshape(bp, seq, d)
    y_sample = y[m_p:].reshape(bs, SAMPLE_ROWS, d)[:, :t_real]
    return (y_prompt, y_sample, *[jnp.stack(v, axis=0) for v in outs_p], *[jnp.stack(v, axis=0) for v in outs_s])
```

```python
import functools
import math

import jax
import jax.numpy as jnp
from jax import lax
from jax.experimental import pallas as pl
from jax.experimental.pallas import tpu as pltpu

HEAD_DIM = 128
ROPE_BASE = 10000.0
EPS = 1e-6
N_BRANCH = 3
SAMPLE_ROWS = 8
SEQ_PER_GROUP = 16
Q_ROWS = 16
PAGES_PER_STEP = 8
CUMSUM_SUB = 256
VMEM_LIMIT = 56 * 1024 * 1024

F32 = jnp.float32
BF16 = jnp.bfloat16


def _cparams(sem):
    return pltpu.CompilerParams(dimension_semantics=sem, vmem_limit_bytes=VMEM_LIMIT)


def _tile(n, pref, align):
    t = min(pref, n)
    t -= t % align
    while t > align and n % t:
        t -= align
    assert t >= align and n % t == 0, (n, pref, align)
    return t


def _dot(a, b):
    return jnp.dot(a, b, preferred_element_type=F32)


def _dot_nt(a, b):
    return lax.dot_general(a, b, (((1,), (1,)), ((), ())), preferred_element_type=F32)


def _split_bf16(x, terms):
    out = []
    r = x
    for _ in range(terms):
        p = r.astype(BF16)
        out.append(p)
        r = r - p.astype(F32)
    return out


def _div_pow2(x, n):
    assert n & (n - 1) == 0
    return x >> (n.bit_length() - 1)


def _sigmoid(x):
    return 1.0 / (1.0 + jnp.exp(-x))


def _log_sigmoid(x):
    return jnp.minimum(x, 0.0) - jnp.log(1.0 + jnp.exp(-jnp.abs(x)))


def _rmsnorm_kernel(x_ref, g_ref, o_ref):
    x = x_ref[...]
    y = x * lax.rsqrt(jnp.mean(x * x, axis=-1, keepdims=True) + EPS)
    o_ref[...] = (y * g_ref[...]).astype(o_ref.dtype)


def _rmsnorm(x, g, out_dtype):
    m, d = x.shape
    tm = _tile(m, 512, 8)
    return pl.pallas_call(
        _rmsnorm_kernel,
        grid=(m // tm,),
        in_specs=[pl.BlockSpec((tm, d), lambda i: (i, 0)), pl.BlockSpec((1, d), lambda i: (0, 0))],
        out_specs=pl.BlockSpec((tm, d), lambda i: (i, 0)),
        out_shape=jax.ShapeDtypeStruct((m, d), out_dtype),
        compiler_params=_cparams(("parallel",)),
        name="rmsnorm",
    )(x, g.reshape(1, d))


def _matmul_kernel(a_ref, w_ref, o_ref):
    o_ref[...] = _dot(a_ref[...], w_ref[...]).astype(o_ref.dtype)


def _matmul(a, w, out_dtype):
    m, k = a.shape
    n = w.shape[1]
    tm, tn = _tile(m, 1024, 16), _tile(n, 1024, 128)
    return pl.pallas_call(
        _matmul_kernel,
        grid=(m // tm, n // tn),
        in_specs=[pl.BlockSpec((tm, k), lambda i, j: (i, 0)), pl.BlockSpec((k, tn), lambda i, j: (0, j))],
        out_specs=pl.BlockSpec((tm, tn), lambda i, j: (i, j)),
        out_shape=jax.ShapeDtypeStruct((m, n), out_dtype),
        compiler_params=_cparams(("parallel", "parallel")),
        name="in_proj",
    )(a, w)


def _forget_kernel(a_ref, w_ref, b_ref, o_ref):
    o_ref[...] = _log_sigmoid(_dot(a_ref[...], w_ref[...]) + b_ref[...])


def _forget_proj(a, w, b):
    m, k = a.shape
    n = w.shape[1]
    tm = _tile(m, 1024, 16)
    return pl.pallas_call(
        _forget_kernel,
        grid=(m // tm,),
        in_specs=[pl.BlockSpec((tm, k), lambda i: (i, 0)), pl.BlockSpec((k, n), lambda i: (0, 0)),
                  pl.BlockSpec((1, n), lambda i: (0, 0))],
        out_specs=pl.BlockSpec((tm, n), lambda i: (i, 0)),
        out_shape=jax.ShapeDtypeStruct((m, n), F32),
        compiler_params=_cparams(("parallel",)),
        name="forget_proj",
    )(a, w, b)


def _merge_kernel(or_ref, of_ref, os_ref, wr_ref, wf_ref, ws_ref, gr_ref, gf_ref, gs_ref,
                  br_ref, bf_ref, bs_ref, o_ref):
    acc = _sigmoid(gr_ref[...] + br_ref[...]) * _dot(or_ref[...], wr_ref[...])
    acc += _sigmoid(gf_ref[...] + bf_ref[...]) * _dot(of_ref[...], wf_ref[...])
    acc += _sigmoid(gs_ref[...] + bs_ref[...]) * _dot(os_ref[...], ws_ref[...])
    o_ref[...] = acc.astype(o_ref.dtype)


def _merge(o_r, o_f, o_s, w_r, w_f, w_s, proj, gate_col0, b_gate):
    m, kk = o_r.shape
    d = w_r.shape[1]
    tm, tn = _tile(m, 512, 16), _tile(d, 1024, 128)
    gblk = gate_col0 // tn
    nblk = d // tn
    o_spec = pl.BlockSpec((tm, kk), lambda i, j: (i, 0))
    w_spec = pl.BlockSpec((kk, tn), lambda i, j: (0, j))

    def g_spec(r):
        return pl.BlockSpec((tm, tn), lambda i, j: (i, gblk + r * nblk + j))

    def b_spec(r):
        return pl.BlockSpec((1, tn), lambda i, j: (0, r * nblk + j))

    b2 = b_gate.reshape(1, N_BRANCH * d)
    return pl.pallas_call(
        _merge_kernel,
        grid=(m // tm, d // tn),
        in_specs=[o_spec, o_spec, o_spec, w_spec, w_spec, w_spec, g_spec(0), g_spec(1), g_spec(2),
                  b_spec(0), b_spec(1), b_spec(2)],
        out_specs=pl.BlockSpec((tm, tn), lambda i, j: (i, j)),
        out_shape=jax.ShapeDtypeStruct((m, d), BF16),
        compiler_params=_cparams(("parallel", "parallel")),
        name="merge",
    )(o_r, o_f, o_s, w_r, w_f, w_s, proj, proj, proj, b2, b2, b2)


def _matmul_resid_kernel(a_ref, w_ref, r_ref, o_ref):
    o_ref[...] = r_ref[...] + _dot(a_ref[...], w_ref[...])


def _matmul_resid(a, w, resid):
    m, k = a.shape
    n = w.shape[1]
    tm, tn = _tile(m, 1024, 16), _tile(n, 1024, 128)
    return pl.pallas_call(
        _matmul_resid_kernel,
        grid=(m // tm, n // tn),
        in_specs=[pl.BlockSpec((tm, k), lambda i, j: (i, 0)), pl.BlockSpec((k, tn), lambda i, j: (0, j)),
                  pl.BlockSpec((tm, tn), lambda i, j: (i, j))],
        out_specs=pl.BlockSpec((tm, tn), lambda i, j: (i, j)),
        out_shape=jax.ShapeDtypeStruct((m, n), F32),
        compiler_params=_cparams(("parallel", "parallel")),
        name="out_proj",
    )(a, w, resid)


def _up_kernel(a_ref, w_ref, o_ref):
    h = jnp.maximum(_dot(a_ref[...], w_ref[...]), 0.0)
    o_ref[...] = (h * h).astype(o_ref.dtype)


def _mlp_up(a, w):
    m, k = a.shape
    n = w.shape[1]
    tm, tn = _tile(m, 1024, 16), _tile(n, 1024, 128)
    return pl.pallas_call(
        _up_kernel,
        grid=(m // tm, n // tn),
        in_specs=[pl.BlockSpec((tm, k), lambda i, j: (i, 0)), pl.BlockSpec((k, tn), lambda i, j: (0, j))],
        out_specs=pl.BlockSpec((tm, tn), lambda i, j: (i, j)),
        out_shape=jax.ShapeDtypeStruct((m, n), BF16),
        compiler_params=_cparams(("parallel", "parallel")),
        name="mlp_up",
    )(a, w)


def _down_kernel(a_ref, w_ref, r_ref, o_ref, acc_ref):
    kk = pl.program_id(2)

    @pl.when(kk == 0)
    def _():
        acc_ref[...] = r_ref[...]

    acc_ref[...] += _dot(a_ref[...], w_ref[...])

    @pl.when(kk == pl.num_programs(2) - 1)
    def _():
        o_ref[...] = acc_ref[...]


def _mlp_down(a, w, resid):
    m, k = a.shape
    n = w.shape[1]
    tm, tn, tk = _tile(m, 1024, 16), _tile(n, 1024, 128), _tile(k, 2048, 128)
    return pl.pallas_call(
        _down_kernel,
        grid=(m // tm, n // tn, k // tk),
        in_specs=[pl.BlockSpec((tm, tk), lambda i, j, kk: (i, kk)), pl.BlockSpec((tk, tn), lambda i, j, kk: (kk, j)),
                  pl.BlockSpec((tm, tn), lambda i, j, kk: (i, j))],
        out_specs=pl.BlockSpec((tm, tn), lambda i, j, kk: (i, j)),
        out_shape=jax.ShapeDtypeStruct((m, n), F32),
        scratch_shapes=[pltpu.VMEM((tm, tn), F32)],
        compiler_params=_cparams(("parallel", "parallel", "arbitrary")),
        name="mlp_down",
    )(a, w, resid)


def _cumsum_kernel(x_ref, o_ref, carry_ref):
    c = pl.program_id(1)

    @pl.when(c == 0)
    def _():
        carry_ref[...] = jnp.zeros_like(carry_ref)

    n = x_ref.shape[1]
    row = lax.broadcasted_iota(jnp.int32, (n, n), 0)
    col = lax.broadcasted_iota(jnp.int32, (n, n), 1)
    tri = jnp.where(col <= row, 1.0, 0.0).astype(BF16)
    acc = carry_ref[...]
    for piece in _split_bf16(x_ref[0], 3):
        acc = acc + _dot(tri, piece)
    o_ref[0] = acc
    carry_ref[...] = jnp.broadcast_to(acc[n - 1:n, :], carry_ref.shape)


def _cumsum_rows(x):
    r, t, lanes = x.shape
    n = 128
    return pl.pallas_call(
        _cumsum_kernel,
        grid=(r, t // n),
        in_specs=[pl.BlockSpec((1, n, lanes), lambda i, c: (i, c, 0))],
        out_specs=pl.BlockSpec((1, n, lanes), lambda i, c: (i, c, 0)),
        out_shape=jax.ShapeDtypeStruct((r, t, lanes), F32),
        scratch_shapes=[pltpu.VMEM((n, lanes), F32)],
        compiler_params=_cparams(("parallel", "arbitrary")),
        name="logf_cumsum",
    )(x)


def _gather_logf_kernel(pt_ref, *refs):
    o_ref = refs[-1]
    for p, x_ref in enumerate(refs[:-1]):
        o_ref[0, p] = x_ref[0, 0]


def _gather_logf(cache_logf, page_table, layer):
    bs, n_pages = page_table.shape
    depth, n_pool, page, h = cache_logf.shape
    flat = cache_logf.reshape(depth, n_pool, 1, page * h)
    in_specs = [pl.BlockSpec((1, 1, 1, page * h), functools.partial(lambda b, pt, p: (layer, pt[b, p], 0, 0), p=p))
                for p in range(n_pages)]
    out = pl.pallas_call(
        _gather_logf_kernel,
        grid_spec=pltpu.PrefetchScalarGridSpec(
            num_scalar_prefetch=1, grid=(bs,), in_specs=in_specs,
            out_specs=pl.BlockSpec((1, n_pages, 1, page * h), lambda b, pt: (b, 0, 0, 0))),
        out_shape=jax.ShapeDtypeStruct((bs, n_pages, 1, page * h), F32),
        compiler_params=_cparams(("arbitrary",)),
        name="gather_logf",
    )(page_table, *([flat] * n_pages))
    return out.reshape(bs, n_pages, page, h)


def _rope(x, cosf, sinf):
    return x * cosf + pltpu.roll(x, HEAD_DIM // 2, 1) * sinf


def _log_gamma(h):
    return math.log(1.0 - 2.0 ** (-5.0 - h))


def _head_norm_gate(o, g):
    on = o * lax.rsqrt(jnp.mean(o * o, axis=-1, keepdims=True) + EPS)
    return on * (g * _sigmoid(g))


def _ret_prompt_kernel(q_ref, k_ref, v_ref, g_ref, cos_ref, sin_ref, o_ref, s_ref, *, n_heads):
    c = q_ref.shape[0]

    @pl.when(pl.program_id(1) == 0)
    def _():
        s_ref[...] = jnp.zeros_like(s_ref)

    cosf, sinf = cos_ref[...], sin_ref[...]
    ii = lax.broadcasted_iota(jnp.int32, (c, c), 0).astype(F32)
    jj = lax.broadcasted_iota(jnp.int32, (c, c), 1).astype(F32)
    diff = ii - jj
    scale = HEAD_DIM ** -0.5
    for h in range(n_heads):
        lg = _log_gamma(h)
        hs = slice(h * HEAD_DIM, (h + 1) * HEAD_DIM)
        inner = jnp.where(diff >= 0, jnp.exp(diff * lg), 0.0)
        cross = jnp.exp((ii + 1.0) * lg)
        kdec = jnp.exp((c - 1.0 - ii) * lg)
        qr = _rope(q_ref[:, hs], cosf, sinf)
        kr = _rope(k_ref[:, hs], cosf, sinf) * scale
        qb, kb, vb = qr.astype(BF16), kr.astype(BF16), v_ref[:, hs].astype(BF16)
        a = _dot_nt(qb, kb) * inner
        s = s_ref[0, h]
        o = _dot(a.astype(BF16), vb) + _dot(qb, s.astype(BF16)) * cross
        s_ref[0, h] = s * math.exp(c * lg) + _dot((kr * kdec).T.astype(BF16), vb)
        o_ref[:, hs] = _head_norm_gate(o, g_ref[:, hs]).astype(o_ref.dtype)


def _ret_prompt(proj, cosf, sinf, batch, seq, n_heads, col0):
    mix = n_heads * HEAD_DIM
    c = 128
    nc = seq // c
    cb = col0 // mix

    def seg(r):
        return pl.BlockSpec((c, mix), lambda b, i: (b * nc + i, cb + r))

    tab = pl.BlockSpec((c, HEAD_DIM), lambda b, i: (i, 0))
    return pl.pallas_call(
        functools.partial(_ret_prompt_kernel, n_heads=n_heads),
        grid=(batch, nc),
        in_specs=[seg(0), seg(1), seg(2), seg(3), tab, tab],
        out_specs=[pl.BlockSpec((c, mix), lambda b, i: (b * nc + i, 0)),
                   pl.BlockSpec((1, n_heads, HEAD_DIM, HEAD_DIM), lambda b, i: (b, 0, 0, 0))],
        out_shape=[jax.ShapeDtypeStruct((batch * seq, mix), BF16),
                   jax.ShapeDtypeStruct((batch, n_heads, HEAD_DIM, HEAD_DIM), F32)],
        compiler_params=_cparams(("parallel", "arbitrary")),
        name="retention_prompt",
    )(proj, proj, proj, proj, cosf, sinf)


def _ret_sample_kernel(q_ref, k_ref, v_ref, g_ref, cos_ref, sin_ref, s0_ref, o_ref, s_ref, *, t_real):
    n = q_ref.shape[0]
    hf = jnp.full((n, n), pl.program_id(1), jnp.int32).astype(F32)
    lg = jnp.log(1.0 - jnp.exp2(-5.0 - hf))
    ri = lax.broadcasted_iota(jnp.int32, (n, n), 0)
    ci = lax.broadcasted_iota(jnp.int32, (n, n), 1)
    rseq, cseq = _div_pow2(ri, SAMPLE_ROWS), _div_pow2(ci, SAMPLE_ROWS)
    il, jl = ri & (SAMPLE_ROWS - 1), ci & (SAMPLE_ROWS - 1)
    ilf = il.astype(F32)
    causal = jnp.logical_and(rseq == cseq, jnp.logical_and(jl <= il, il < t_real))
    inner = jnp.where(causal, jnp.exp((il - jl).astype(F32) * lg), 0.0)
    cross = jnp.exp((ilf + 1.0) * lg)
    kdec = jnp.where(il < t_real, jnp.exp((t_real - 1.0 - ilf) * lg), 0.0)
    cdec = jnp.exp(float(t_real) * lg)
    cosf, sinf = cos_ref[...], sin_ref[...]
    qr = _rope(q_ref[...], cosf, sinf)
    kr = _rope(k_ref[...], cosf, sinf) * (HEAD_DIM ** -0.5)
    qb, kb, vb = qr.astype(BF16), kr.astype(BF16), v_ref[...].astype(BF16)
    o = _dot((_dot_nt(qb, kb) * inner).astype(BF16), vb)
    kdt = (kr * kdec).T
    for sq in range(n // SAMPLE_ROWS):
        s = s0_ref[sq, 0]
        o = jnp.where(rseq == sq, o + _dot(qb, s.astype(BF16)) * cross, o)
        s_ref[sq, 0] = s * cdec + _dot(jnp.where(cseq == sq, kdt, 0.0).astype(BF16), vb)
    o_ref[...] = _head_norm_gate(o, g_ref[...]).astype(o_ref.dtype)


def _ret_sample(proj, cosf, sinf, s0, row0, t_real, col0):
    bs, n_heads = s0.shape[:2]
    mix = n_heads * HEAD_DIM
    n = SEQ_PER_GROUP * SAMPLE_ROWS
    rb = row0 // n
    cb = col0 // HEAD_DIM

    def seg(r):
        return pl.BlockSpec((n, HEAD_DIM), lambda gidx, h: (rb + gidx, cb + r * n_heads + h))

    tab = pl.BlockSpec((n, HEAD_DIM), lambda gidx, h: (0, 0))
    st = pl.BlockSpec((SEQ_PER_GROUP, 1, HEAD_DIM, HEAD_DIM), lambda gidx, h: (gidx, h, 0, 0))
    return pl.pallas_call(
        functools.partial(_ret_sample_kernel, t_real=t_real),
        grid=(bs // SEQ_PER_GROUP, n_heads),
        in_specs=[seg(0), seg(1), seg(2), seg(3), tab, tab, st],
        out_specs=[pl.BlockSpec((n, HEAD_DIM), lambda gidx, h: (gidx, h)), st],
        out_shape=[jax.ShapeDtypeStruct((bs * SAMPLE_ROWS, mix), BF16),
                   jax.ShapeDtypeStruct(s0.shape, F32)],
        compiler_params=_cparams(("parallel", "arbitrary")),
        name="retention_sample",
    )(proj, proj, proj, proj, cosf, sinf, s0)


def _pair_tables(nq, descending):
    qi, ki = [], []
    for q in range(nq):
        ks = range(q, -1, -1) if descending else range(q + 1)
        for k in ks:
            qi.append(q)
            ki.append(k)
    return jnp.asarray(qi, jnp.int32), jnp.asarray(ki, jnp.int32)


def _fox_prompt_kernel(qi_ref, ki_ref, q_ref, k_ref, v_ref, nc_ref, o_ref, m_ref, l_ref, acc_ref, *, n_heads):
    p = pl.program_id(1)
    qi, ki = qi_ref[p], ki_ref[p]
    tq, tk = q_ref.shape[0], k_ref.shape[0]

    @pl.when(ki == 0)
    def _():
        m_ref[...] = jnp.full_like(m_ref, -jnp.inf)
        l_ref[...] = jnp.zeros_like(l_ref)
        acc_ref[...] = jnp.zeros_like(acc_ref)

    scale = HEAD_DIM ** -0.5

    def step(diagonal):
        if diagonal:
            valid = (lax.broadcasted_iota(jnp.int32, (tq, tk), 1) <= lax.broadcasted_iota(jnp.int32, (tq, tk), 0))
        for h in range(n_heads):
            hs = slice(h * HEAD_DIM, (h + 1) * HEAD_DIM)
            qb, kb, vb = q_ref[:, hs].astype(BF16), k_ref[:, hs].astype(BF16), v_ref[:, hs].astype(BF16)
            s = _dot_nt(qb, kb) * scale + nc_ref[0, h]
            if diagonal:
                s = jnp.where(valid, s, -jnp.inf)
            m_prev = m_ref[h]
            m_new = jnp.maximum(m_prev, jnp.max(s, axis=-1, keepdims=True))
            alpha = jnp.exp(m_prev - m_new)
            pr = jnp.exp(s - m_new)
            l_ref[h] = alpha * l_ref[h] + jnp.sum(pr, axis=-1, keepdims=True)
            acc_ref[h] = alpha * acc_ref[h] + _dot(pr.astype(BF16), vb)
            m_ref[h] = m_new

    @pl.when(ki < qi)
    def _():
        step(False)

    @pl.when(ki == qi)
    def _():
        step(True)
        for h in range(n_heads):
            o_ref[:, h * HEAD_DIM:(h + 1) * HEAD_DIM] = (acc_ref[h] / l_ref[h]).astype(o_ref.dtype)


def _sb_prompt_kernel(qi_ref, ki_ref, q_ref, k_ref, v_ref, o_ref, carry_ref, acc_ref, *, n_heads):
    p = pl.program_id(1)
    qi, ki = qi_ref[p], ki_ref[p]
    tq, tk = q_ref.shape[0], k_ref.shape[0]
    sub = min(CUMSUM_SUB, tk)

    @pl.when(ki == qi)
    def _():
        carry_ref[...] = jnp.zeros_like(carry_ref)
        acc_ref[...] = jnp.zeros_like(acc_ref)

    ur = lax.broadcasted_iota(jnp.int32, (sub, sub), 0)
    uc = lax.broadcasted_iota(jnp.int32, (sub, sub), 1)
    upper = jnp.where(ur >= uc, 1.0, 0.0).astype(BF16)
    scale = HEAD_DIM ** -0.5

    def step(diagonal):
        if diagonal:
            rows = lax.broadcasted_iota(jnp.int32, (tq, sub), 0)
            col0 = lax.broadcasted_iota(jnp.int32, (tq, sub), 1)
        for h in range(n_heads):
            hs = slice(h * HEAD_DIM, (h + 1) * HEAD_DIM)
            qb = q_ref[:, hs].astype(BF16)
            carry = carry_ref[h]
            acc = acc_ref[h]
            for sb in range(tk // sub - 1, -1, -1):
                ks = slice(sb * sub, (sb + 1) * sub)
                kb, vb = k_ref[ks, hs].astype(BF16), v_ref[ks, hs].astype(BF16)
                z = _dot_nt(qb, kb) * scale
                lneg = -(jnp.maximum(z, 0.0) + jnp.log(1.0 + jnp.exp(-jnp.abs(z))))
                if diagonal:
                    valid = col0 + sb * sub < rows
                    lneg = jnp.where(valid, lneg, 0.0)
                hi, lo = _split_bf16(lneg, 2)
                incl = _dot(hi, upper) + _dot(lo, upper)
                ex = z + incl + carry
                if diagonal:
                    ex = jnp.where(valid, ex, -1e30)
                acc = acc + _dot(jnp.exp(ex).astype(BF16), vb)
                carry = carry + incl[:, 0:1]
            carry_ref[h] = carry
            acc_ref[h] = acc

    @pl.when(ki == qi)
    def _():
        step(True)

    @pl.when(ki < qi)
    def _():
        step(False)

    @pl.when(ki == 0)
    def _():
        for h in range(n_heads):
            o_ref[:, h * HEAD_DIM:(h + 1) * HEAD_DIM] = acc_ref[h].astype(o_ref.dtype)


def _prompt_attention(kind, proj, batch, seq, n_heads, col0, neg_cum=None):
    mix = n_heads * HEAD_DIM
    t = _tile(seq, 512, 128)
    nq = seq // t
    cb = col0 // mix
    qi, ki = _pair_tables(nq, descending=(kind == "sb"))
    q_spec = pl.BlockSpec((t, mix), lambda b, p, qt, kt: (b * nq + qt[p], cb))
    k_spec = pl.BlockSpec((t, mix), lambda b, p, qt, kt: (b * nq + kt[p], cb + 1))
    v_spec = pl.BlockSpec((t, mix), lambda b, p, qt, kt: (b * nq + kt[p], cb + 2))
    o_spec = pl.BlockSpec((t, mix), lambda b, p, qt, kt: (b * nq + qt[p], 0))
    col_state = pltpu.VMEM((n_heads, t, 1), F32)
    acc_state = pltpu.VMEM((n_heads, t, HEAD_DIM), F32)
    if kind == "fox":
        body = functools.partial(_fox_prompt_kernel, n_heads=n_heads)
        in_specs = [q_spec, k_spec, v_spec,
                    pl.BlockSpec((1, n_heads, 1, t), lambda b, p, qt, kt: (b, 0, 0, kt[p]))]
        args = (proj, proj, proj, neg_cum)
        scratch = [col_state, col_state, acc_state]
    else:
        body = functools.partial(_sb_prompt_kernel, n_heads=n_heads)
        in_specs = [q_spec, k_spec, v_spec]
        args = (proj, proj, proj)
        scratch = [col_state, acc_state]
    return pl.pallas_call(
        body,
        grid_spec=pltpu.PrefetchScalarGridSpec(
            num_scalar_prefetch=2, grid=(batch, int(qi.shape[0])), in_specs=in_specs, out_specs=o_spec,
            scratch_shapes=scratch),
        out_shape=jax.ShapeDtypeStruct((batch * seq, mix), BF16),
        compiler_params=_cparams(("parallel", "arbitrary")),
        name=kind + "_prompt",
    )(qi, ki, *args)


def _page_heads(refs, n_heads):
    page = refs[0].shape[2] // n_heads
    out = []
    for h in range(n_heads):
        rows = [r[0, 0, pl.ds(h, page, stride=n_heads), :] for r in refs]
        out.append(jnp.concatenate(rows, axis=0).astype(BF16))
    return out


def _new_heads(x_ref, pad_ref, n_heads):
    pad_ref[0:SAMPLE_ROWS, :] = x_ref[...]
    return [pad_ref[:, h * HEAD_DIM:(h + 1) * HEAD_DIM].astype(BF16) for h in range(n_heads)]


def _stacked_scores(q_ref, k_heads):
    z = [_dot_nt(q_ref[0, h], kb)[0:SAMPLE_ROWS] for h, kb in enumerate(k_heads)]
    return jnp.concatenate(z, axis=0) * (HEAD_DIM ** -0.5)


def _stacked_values(p, v_heads):
    zeros = jnp.zeros((Q_ROWS - SAMPLE_ROWS, p.shape[1]), F32)
    out = []
    for h, vb in enumerate(v_heads):
        ph = jnp.concatenate([p[h * SAMPLE_ROWS:(h + 1) * SAMPLE_ROWS], zeros], axis=0).astype(BF16)
        out.append(_dot(ph, vb)[0:SAMPLE_ROWS])
    return jnp.concatenate(out, axis=0)


def _new_key_mask(n_rows, page, t_real, strict):
    qidx = lax.broadcasted_iota(jnp.int32, (n_rows, page), 0) & (SAMPLE_ROWS - 1)
    kidx = lax.broadcasted_iota(jnp.int32, (n_rows, page), 1)
    causal = kidx < qidx if strict else kidx <= qidx
    return jnp.logical_and(causal, kidx < t_real)


def _store_heads(o_ref, out, t_real, n_heads):
    real = (lax.broadcasted_iota(jnp.int32, out.shape, 0) & (SAMPLE_ROWS - 1)) < t_real
    out = jnp.where(real, out, 0.0)
    for h in range(n_heads):
        o_ref[0, :, h * HEAD_DIM:(h + 1) * HEAD_DIM] = out[h * SAMPLE_ROWS:(h + 1) * SAMPLE_ROWS]


def _fox_sample_kernel(pt_ref, q_ref, kn_ref, vn_ref, nc_ref, ncn_ref, *refs, n_heads, t_real):
    k_pages, v_pages = refs[:PAGES_PER_STEP], refs[PAGES_PER_STEP:2 * PAGES_PER_STEP]
    o_ref, m_ref, l_ref, acc_ref, kpad_ref, vpad_ref = refs[2 * PAGES_PER_STEP:]
    g = pl.program_id(1)
    n_rows = n_heads * SAMPLE_ROWS

    @pl.when(jnp.logical_and(pl.program_id(0) == 0, g == 0))
    def _():
        kpad_ref[...] = jnp.zeros_like(kpad_ref)
        vpad_ref[...] = jnp.zeros_like(vpad_ref)

    @pl.when(g == 0)
    def _():
        m_ref[...] = jnp.full_like(m_ref, -jnp.inf)
        l_ref[...] = jnp.zeros_like(l_ref)
        acc_ref[...] = jnp.zeros_like(acc_ref)

    def block(k_heads, v_heads, bias_ref, valid):
        nk = bias_ref.shape[3]
        bias = jnp.concatenate([jnp.broadcast_to(bias_ref[0, h], (SAMPLE_ROWS, nk)) for h in range(n_heads)], axis=0)
        s = _stacked_scores(q_ref, k_heads) + bias
        if valid is not None:
            s = jnp.where(valid, s, -jnp.inf)
        m_prev = m_ref[...]
        m_new = jnp.maximum(m_prev, jnp.max(s, axis=-1, keepdims=True))
        alpha = jnp.exp(m_prev - m_new)
        pr = jnp.exp(s - m_new)
        l_ref[...] = alpha * l_ref[...] + jnp.sum(pr, axis=-1, keepdims=True)
        acc_ref[...] = alpha * acc_ref[...] + _stacked_values(pr, v_heads)
        m_ref[...] = m_new

    block(_page_heads(k_pages, n_heads), _page_heads(v_pages, n_heads), nc_ref, None)

    @pl.when(g == pl.num_programs(1) - 1)
    def _():
        page = kpad_ref.shape[0]
        block(_new_heads(kn_ref, kpad_ref, n_heads), _new_heads(vn_ref, vpad_ref, n_heads), ncn_ref,
              _new_key_mask(n_rows, page, t_real, strict=False))
        _store_heads(o_ref, acc_ref[...] / l_ref[...], t_real, n_heads)


def _sb_sample_kernel(pt_ref, q_ref, kn_ref, vn_ref, *refs, n_heads, t_real):
    k_pages, v_pages = refs[:PAGES_PER_STEP], refs[PAGES_PER_STEP:2 * PAGES_PER_STEP]
    o_ref, carry_ref, acc_ref, kpad_ref, vpad_ref = refs[2 * PAGES_PER_STEP:]
    g = pl.program_id(1)
    n_rows = n_heads * SAMPLE_ROWS

    @pl.when(jnp.logical_and(pl.program_id(0) == 0, g == 0))
    def _():
        kpad_ref[...] = jnp.zeros_like(kpad_ref)
        vpad_ref[...] = jnp.zeros_like(vpad_ref)

    def block(k_heads, v_heads, valid, sub):
        z = _stacked_scores(q_ref, k_heads)
        nsub = z.shape[1] // sub
        ur = lax.broadcasted_iota(jnp.int32, (sub, sub), 0)
        uc = lax.broadcasted_iota(jnp.int32, (sub, sub), 1)
        upper = jnp.where(ur >= uc, 1.0, 0.0).astype(BF16)
        lneg = -(jnp.maximum(z, 0.0) + jnp.log(1.0 + jnp.exp(-jnp.abs(z))))
        if valid is not None:
            lneg = jnp.where(valid, lneg, 0.0)
        hi, lo = _split_bf16(lneg, 2)
        incl = []
        for j in range(nsub):
            js = slice(j * sub, (j + 1) * sub)
            r = _dot(jnp.concatenate([hi[:, js], lo[:, js]], axis=0), upper)
            incl.append(r[:n_rows] + r[n_rows:])
        carry = carry_ref[...]
        a = [None] * nsub
        for j in range(nsub - 1, -1, -1):
            ex = z[:, j * sub:(j + 1) * sub] + incl[j] + carry
            if valid is not None:
                ex = jnp.where(valid[:, j * sub:(j + 1) * sub], ex, -1e30)
            a[j] = jnp.exp(ex)
            carry = carry + incl[j][:, 0:1]
        carry_ref[...] = carry
        acc_ref[...] += _stacked_values(jnp.concatenate(a, axis=1), v_heads)

    @pl.when(g == 0)
    def _():
        carry_ref[...] = jnp.zeros_like(carry_ref)
        acc_ref[...] = jnp.zeros_like(acc_ref)
        page = kpad_ref.shape[0]
        block(_new_heads(kn_ref, kpad_ref, n_heads), _new_heads(vn_ref, vpad_ref, n_heads),
              _new_key_mask(n_rows, page, t_real, strict=True), page)

    block(_page_heads(k_pages, n_heads), _page_heads(v_pages, n_heads), None, CUMSUM_SUB)

    @pl.when(g == pl.num_programs(1) - 1)
    def _():
        _store_heads(o_ref, acc_ref[...], t_real, n_heads)


def _sample_attention(kind, q, proj, cache_k, cache_v, page_table, layer, row0, col0, t_real, neg_cum=None):
    bs, n_pages = page_table.shape
    depth, n_pool, page, n_heads, _ = cache_k.shape
    mix = n_heads * HEAD_DIM
    n_rows = n_heads * SAMPLE_ROWS
    n_steps = n_pages // PAGES_PER_STEP
    assert n_pages % PAGES_PER_STEP == 0 and (PAGES_PER_STEP * page) % CUMSUM_SUB == 0
    ck = cache_k.reshape(depth, n_pool, page * n_heads, HEAD_DIM)
    cv = cache_v.reshape(depth, n_pool, page * n_heads, HEAD_DIM)
    rb = row0 // SAMPLE_ROWS
    cb = col0 // mix
    descending = kind == "sb"

    def page_spec(pp):
        def imap(b, g, pt):
            gg = n_steps - 1 - g if descending else g
            return (layer, pt[b, gg * PAGES_PER_STEP + pp], 0, 0)
        return pl.BlockSpec((1, 1, page * n_heads, HEAD_DIM), imap)

    q_spec = pl.BlockSpec((1, n_heads, Q_ROWS, HEAD_DIM), lambda b, g, pt: (b, 0, 0, 0))
    kn_spec = pl.BlockSpec((SAMPLE_ROWS, mix), lambda b, g, pt: (rb + b, cb + 1))
    vn_spec = pl.BlockSpec((SAMPLE_ROWS, mix), lambda b, g, pt: (rb + b, cb + 2))
    page_specs = [page_spec(pp) for pp in range(PAGES_PER_STEP)]
    o_spec = pl.BlockSpec((1, SAMPLE_ROWS, mix), lambda b, g, pt: (b, 0, 0))
    col_state = pltpu.VMEM((n_rows, 1), F32)
    acc_state = pltpu.VMEM((n_rows, HEAD_DIM), F32)
    pad_state = pltpu.VMEM((page, mix), F32)
    if kind == "fox":
        body = functools.partial(_fox_sample_kernel, n_heads=n_heads, t_real=t_real)
        in_specs = [q_spec, kn_spec, vn_spec,
                    pl.BlockSpec((1, n_heads, 1, PAGES_PER_STEP * page), lambda b, g, pt: (b, 0, 0, g)),
                    pl.BlockSpec((1, n_heads, 1, page), lambda b, g, pt: (b, 0, 0, n_pages))]
        args = (q, proj, proj, neg_cum, neg_cum)
        scratch = [col_state, col_state, acc_state, pad_state, pad_state]
    else:
        body = functools.partial(_sb_sample_kernel, n_heads=n_heads, t_real=t_real)
        in_specs = [q_spec, kn_spec, vn_spec]
        args = (q, proj, proj)
        scratch = [col_state, acc_state, pad_state, pad_state]
    return pl.pallas_call(
        body,
        grid_spec=pltpu.PrefetchScalarGridSpec(
            num_scalar_prefetch=1, grid=(bs, n_steps), in_specs=in_specs + page_specs + page_specs,
            out_specs=o_spec, scratch_shapes=scratch),
        out_shape=jax.ShapeDtypeStruct((bs, SAMPLE_ROWS, mix), F32),
        compiler_params=_cparams(("arbitrary", "arbitrary")),
        name=kind + "_sample",
    )(page_table, *args, *([ck] * PAGES_PER_STEP), *([cv] * PAGES_PER_STEP))


def _rope_tables(pos):
    half = HEAD_DIM // 2
    inv = ROPE_BASE ** (-jnp.arange(half, dtype=F32) / half)
    ang = pos.astype(F32)[:, None] * inv[None, :]
    cos, sin = jnp.cos(ang), jnp.sin(ang)
    return jnp.concatenate([cos, cos], axis=-1), jnp.concatenate([-sin, sin], axis=-1)


def _head_major_queries(q, t_real):
    qt = jnp.transpose(q[:, :t_real], (0, 2, 1, 3)).astype(BF16)
    return jnp.pad(qt, ((0, 0), (0, 0), (0, Q_ROWS - t_real), (0, 0)))


def kernel(x_prompt, x_sample, cache_fox_k, cache_fox_v, cache_fox_logf, cache_sb_k, cache_sb_v, state_ret,
           page_table, w_in, b_gate, b_forget, w_ret_o, w_fox_o, w_sb_o, w_out, w_up, w_down, g_mix, g_mlp,
           g_final):
    bp, seq, d = x_prompt.shape
    bs, t_real, _ = x_sample.shape
    depth, _, page, n_heads, _ = cache_fox_k.shape
    n_pages = page_table.shape[1]
    past = n_pages * page
    mix = n_heads * HEAD_DIM
    m_p = bp * seq
    m_s = bs * SAMPLE_ROWS
    assert t_real <= SAMPLE_ROWS and bs % SEQ_PER_GROUP == 0 and page == 128 and seq % 128 == 0
    c_ret, c_fox, c_sb, c_gate = 0, 4 * mix, 7 * mix, 10 * mix
    f0 = 7 * mix

    xs = jnp.pad(x_sample, ((0, 0), (0, SAMPLE_ROWS - t_real), (0, 0)))
    x = jnp.concatenate([x_prompt.reshape(m_p, d), xs.reshape(m_s, d)], axis=0)

    cos_p, sin_p = _rope_tables(jnp.arange(seq))
    cos_s, sin_s = _rope_tables(past + jnp.arange(SAMPLE_ROWS))
    cos_s, sin_s = jnp.tile(cos_s, (SEQ_PER_GROUP, 1)), jnp.tile(sin_s, (SEQ_PER_GROUP, 1))
    lane_pad = 128 - n_heads

    outs_p = [[] for _ in range(6)]
    outs_s = [[] for _ in range(6)]
    for layer in range(depth):
        w_main = jnp.concatenate([w_in[layer, :, :f0], w_in[layer, :, f0 + n_heads:]], axis=1).astype(BF16)
        w_fg = jnp.pad(w_in[layer, :, f0:f0 + n_heads], ((0, 0), (0, lane_pad))).astype(BF16)
        b_fg = jnp.pad(b_forget[layer], (0, lane_pad)).reshape(1, 128)

        xn = _rmsnorm(x, g_mix[layer], BF16)
        proj = _matmul(xn, w_main, F32)
        logf = _forget_proj(xn, w_fg, b_fg)

        o_r_p, ret_p = _ret_prompt(proj, cos_p, sin_p, bp, seq, n_heads, c_ret)
        o_r_s, ret_s = _ret_sample(proj, cos_s, sin_s, state_ret[layer], m_p, t_real, c_ret)

        cum_p = _cumsum_rows(logf[:m_p].reshape(bp, seq, 128))
        neg_cum_p = -jnp.transpose(cum_p[:, :, :n_heads], (0, 2, 1)).reshape(bp, n_heads, 1, seq)
        o_f_p = _prompt_attention("fox", proj, bp, seq, n_heads, c_fox, neg_cum_p)

        logf_s = logf[m_p:].reshape(bs, SAMPLE_ROWS, 128)[:, :, :n_heads]
        logf_s = jnp.where(jnp.arange(SAMPLE_ROWS)[None, :, None] < t_real, logf_s, 0.0)
        logf_past = _gather_logf(cache_fox_logf, page_table, layer).reshape(bs, past, n_heads)
        total = past + page
        lf = jnp.concatenate([logf_past, logf_s, jnp.zeros((bs, page - SAMPLE_ROWS, n_heads), F32)], axis=1)
        n_grp = bs * n_heads // 128
        lf = jnp.transpose(lf.reshape(n_grp, 128 // n_heads, total, n_heads), (0, 2, 1, 3)).reshape(n_grp, total, 128)
        cum_s = _cumsum_rows(lf)
        cum_s = jnp.transpose(cum_s.reshape(n_grp, total, 128 // n_heads, n_heads), (0, 2, 3, 1))
        neg_cum_s = -cum_s.reshape(bs, n_heads, 1, total)
        q_f_s = proj[m_p:, c_fox:c_fox + mix].reshape(bs, SAMPLE_ROWS, n_heads, HEAD_DIM)
        o_f_s = _sample_attention("fox", _head_major_queries(q_f_s, t_real), proj, cache_fox_k, cache_fox_v,
                                  page_table, layer, m_p, c_fox, t_real, neg_cum_s)

        o_s_p = _prompt_attention("sb", proj, bp, seq, n_heads, c_sb)
        q_s_s = proj[m_p:, c_sb:c_sb + mix].reshape(bs, SAMPLE_ROWS, n_heads, HEAD_DIM)
        o_s_s = _sample_attention("sb", _head_major_queries(q_s_s, t_real), proj, cache_sb_k, cache_sb_v,
                                  page_table, layer, m_p, c_sb, t_real)

        o_r = jnp.concatenate([o_r_p, o_r_s], axis=0)
        o_f = jnp.concatenate([o_f_p, o_f_s.reshape(m_s, mix).astype(BF16)], axis=0)
        o_s = jnp.concatenate([o_s_p, o_s_s.reshape(m_s, mix).astype(BF16)], axis=0)

        merged = _merge(o_r, o_f, o_s, w_ret_o[layer].astype(BF16), w_fox_o[layer].astype(BF16),
                        w_sb_o[layer].astype(BF16), proj, c_gate, b_gate[layer])
        x = _matmul_resid(merged, w_out[layer].astype(BF16), x)
        hmid = _mlp_up(_rmsnorm(x, g_mlp[layer], BF16), w_up[layer].astype(BF16))
        x = _mlp_down(hmid, w_down[layer].astype(BF16), x)

        def seg_p(c0):
            return proj[:m_p, c0:c0 + mix].reshape(bp, seq, n_heads, HEAD_DIM)

        def seg_s(c0):
            return proj[m_p:, c0:c0 + mix].reshape(bs, SAMPLE_ROWS, n_heads, HEAD_DIM)[:, :t_real]

        for lst, val in zip(outs_p, (seg_p(c_fox + mix), seg_p(c_fox + 2 * mix),
                                     logf[:m_p, :n_heads].reshape(bp, seq, n_heads),
                                     seg_p(c_sb + mix), seg_p(c_sb + 2 * mix), ret_p)):
            lst.append(val)
        for lst, val in zip(outs_s, (seg_s(c_fox + mix), seg_s(c_fox + 2 * mix),
                                     logf[m_p:, :n_heads].reshape(bs, SAMPLE_ROWS, n_heads)[:, :t_real],
                                     seg_s(c_sb + mix), seg_s(c_sb + 2 * mix), ret_s)):
            lst.append(val)

    y = _rmsnorm(x, g_final, F32)
    y_prompt = y[:m_p].reshape(bp, seq, d)
    y_sample = y[m_p:].reshape(bs, SAMPLE_ROWS, d)[:, :t_real]
    return (y_prompt, y_sample, *[jnp.stack(v, axis=0) for v in outs_p], *[jnp.stack(v, axis=0) for v in outs_s])
```

```python
import functools
import math

import jax
import jax.numpy as jnp
from jax import lax
from jax.experimental import pallas as pl
from jax.experimental.pallas import tpu as pltpu

HEAD_DIM = 128
ROPE_BASE = 10000.0
EPS = 1e-6
N_BRANCH = 3
SAMPLE_ROWS = 8
SEQ_PER_GROUP = 16
Q_ROWS = 16
PAGES_PER_STEP = 8
CUMSUM_SUB = 256
VMEM_LIMIT = 56 * 1024 * 1024

F32 = jnp.float32
BF16 = jnp.bfloat16


def _cparams(sem):
    return pltpu.CompilerParams(dimension_semantics=sem, vmem_limit_bytes=VMEM_LIMIT)


def _tile(n, pref, align):
    t = min(pref, n)
    t -= t % align
    while t > align and n % t:
        t -= align
    assert t >= align and n % t == 0, (n, pref, align)
    return t


def _dot(a, b):
    return jnp.dot(a, b, preferred_element_type=F32)


def _dot_nt(a, b):
    return lax.dot_general(a, b, (((1,), (1,)), ((), ())), preferred_element_type=F32)


def _split_bf16(x, terms):
    out = []
    r = x
    for _ in range(terms):
        p = r.astype(BF16)
        out.append(p)
        r = r - p.astype(F32)
    return out


def _div_pow2(x, n):
    assert n & (n - 1) == 0
    return x >> (n.bit_length() - 1)


def _sigmoid(x):
    return 1.0 / (1.0 + jnp.exp(-x))


def _log_sigmoid(x):
    return jnp.minimum(x, 0.0) - jnp.log(1.0 + jnp.exp(-jnp.abs(x)))


def _rmsnorm_kernel(x_ref, g_ref, o_ref):
    x = x_ref[...]
    y = x * lax.rsqrt(jnp.mean(x * x, axis=-1, keepdims=True) + EPS)
    o_ref[...] = (y * g_ref[...]).astype(o_ref.dtype)


def _rmsnorm(x, g, out_dtype):
    m, d = x.shape
    tm = _tile(m, 512, 8)
    return pl.pallas_call(
        _rmsnorm_kernel,
        grid=(m // tm,),
        in_specs=[pl.BlockSpec((tm, d), lambda i: (i, 0)), pl.BlockSpec((1, d), lambda i: (0, 0))],
        out_specs=pl.BlockSpec((tm, d), lambda i: (i, 0)),
        out_shape=jax.ShapeDtypeStruct((m, d), out_dtype),
        compiler_params=_cparams(("parallel",)),
        name="rmsnorm",
    )(x, g.reshape(1, d))


def _matmul_kernel(a_ref, w_ref, o_ref):
    o_ref[...] = _dot(a_ref[...], w_ref[...]).astype(o_ref.dtype)


def _matmul(a, w, out_dtype):
    m, k = a.shape
    n = w.shape[1]
    tm, tn = _tile(m, 1024, 16), _tile(n, 1024, 128)
    return pl.pallas_call(
        _matmul_kernel,
        grid=(m // tm, n // tn),
        in_specs=[pl.BlockSpec((tm, k), lambda i, j: (i, 0)), pl.BlockSpec((k, tn), lambda i, j: (0, j))],
        out_specs=pl.BlockSpec((tm, tn), lambda i, j: (i, j)),
        out_shape=jax.ShapeDtypeStruct((m, n), out_dtype),
        compiler_params=_cparams(("parallel", "parallel")),
        name="in_proj",
    )(a, w)


def _forget_kernel(a_ref, w_ref, b_ref, o_ref):
    o_ref[...] = _log_sigmoid(_dot(a_ref[...], w_ref[...]) + b_ref[...])


def _forget_proj(a, w, b):
    m, k = a.shape
    n = w.shape[1]
    tm = _tile(m, 1024, 16)
    return pl.pallas_call(
        _forget_kernel,
        grid=(m // tm,),
        in_specs=[pl.BlockSpec((tm, k), lambda i: (i, 0)), pl.BlockSpec((k, n), lambda i: (0, 0)),
                  pl.BlockSpec((1, n), lambda i: (0, 0))],
        out_specs=pl.BlockSpec((tm, n), lambda i: (i, 0)),
        out_shape=jax.ShapeDtypeStruct((m, n), F32),
        compiler_params=_cparams(("parallel",)),
        name="forget_proj",
    )(a, w, b)


def _merge_kernel(or_ref, of_ref, os_ref, wr_ref, wf_ref, ws_ref, gr_ref, gf_ref, gs_ref,
                  br_ref, bf_ref, bs_ref, o_ref):
    acc = _sigmoid(gr_ref[...] + br_ref[...]) * _dot(or_ref[...], wr_ref[...])
    acc += _sigmoid(gf_ref[...] + bf_ref[...]) * _dot(of_ref[...], wf_ref[...])
    acc += _sigmoid(gs_ref[...] + bs_ref[...]) * _dot(os_ref[...], ws_ref[...])
    o_ref[...] = acc.astype(o_ref.dtype)


def _merge(o_r, o_f, o_s, w_r, w_f, w_s, proj, gate_col0, b_gate):
    m, kk = o_r.shape
    d = w_r.shape[1]
    tm, tn = _tile(m, 512, 16), _tile(d, 1024, 128)
    gblk = gate_col0 // tn
    nblk = d // tn
    o_spec = pl.BlockSpec((tm, kk), lambda i, j: (i, 0))
    w_spec = pl.BlockSpec((kk, tn), lambda i, j: (0, j))

    def g_spec(r):
        return pl.BlockSpec((tm, tn), lambda i, j: (i, gblk + r * nblk + j))

    def b_spec(r):
        return pl.BlockSpec((1, tn), lambda i, j: (0, r * nblk + j))

    b2 = b_gate.reshape(1, N_BRANCH * d)
    return pl.pallas_call(
        _merge_kernel,
        grid=(m // tm, d // tn),
        in_specs=[o_spec, o_spec, o_spec, w_spec, w_spec, w_spec, g_spec(0), g_spec(1), g_spec(2),
                  b_spec(0), b_spec(1), b_spec(2)],
        out_specs=pl.BlockSpec((tm, tn), lambda i, j: (i, j)),
        out_shape=jax.ShapeDtypeStruct((m, d), BF16),
        compiler_params=_cparams(("parallel", "parallel")),
        name="merge",
    )(o_r, o_f, o_s, w_r, w_f, w_s, proj, proj, proj, b2, b2, b2)


def _matmul_resid_kernel(a_ref, w_ref, r_ref, o_ref):
    o_ref[...] = r_ref[...] + _dot(a_ref[...], w_ref[...])


def _matmul_resid(a, w, resid):
    m, k = a.shape
    n = w.shape[1]
    tm, tn = _tile(m, 1024, 16), _tile(n, 1024, 128)
    return pl.pallas_call(
        _matmul_resid_kernel,
        grid=(m // tm, n // tn),
        in_specs=[pl.BlockSpec((tm, k), lambda i, j: (i, 0)), pl.BlockSpec((k, tn), lambda i, j: (0, j)),
                  pl.BlockSpec((tm, tn), lambda i, j: (i, j))],
        out_specs=pl.BlockSpec((tm, tn), lambda i, j: (i, j)),
        out_shape=jax.ShapeDtypeStruct((m, n), F32),
        compiler_params=_cparams(("parallel", "parallel")),
        name="out_proj",
    )(a, w, resid)


def _up_kernel(a_ref, w_ref, o_ref):
    h = jnp.maximum(_dot(a_ref[...], w_ref[...]), 0.0)
    o_ref[...] = (h * h).astype(o_ref.dtype)


def _mlp_up(a, w):
    m, k = a.shape
    n = w.shape[1]
    tm, tn = _tile(m, 1024, 16), _tile(n, 1024, 128)
    return pl.pallas_call(
        _up_kernel,
        grid=(m // tm, n // tn),
        in_specs=[pl.BlockSpec((tm, k), lambda i, j: (i, 0)), pl.BlockSpec((k, tn), lambda i, j: (0, j))],
        out_specs=pl.BlockSpec((tm, tn), lambda i, j: (i, j)),
        out_shape=jax.ShapeDtypeStruct((m, n), BF16),
        compiler_params=_cparams(("parallel", "parallel")),
        name="mlp_up",
    )(a, w)


def _down_kernel(a_ref, w_ref, r_ref, o_ref, acc_ref):
    kk = pl.program_id(2)

    @pl.when(kk == 0)
    def _():
        acc_ref[...] = r_ref[...]

    acc_ref[...] += _dot(a_ref[...], w_ref[...])

    @pl.when(kk == pl.num_programs(2) - 1)
    def _():
        o_ref[...] = acc_ref[...]


def _mlp_down(a, w, resid):
    m, k = a.shape
    n = w.shape[1]
    tm, tn, tk = _tile(m, 1024, 16), _tile(n, 1024, 128), _tile(k, 2048, 128)
    return pl.pallas_call(
        _down_kernel,
        grid=(m // tm, n // tn, k // tk),
        in_specs=[pl.BlockSpec((tm, tk), lambda i, j, kk: (i, kk)), pl.BlockSpec((tk, tn), lambda i, j, kk: (kk, j)),
                  pl.BlockSpec((tm, tn), lambda i, j, kk: (i, j))],
        out_specs=pl.BlockSpec((tm, tn), lambda i, j, kk: (i, j)),
        out_shape=jax.ShapeDtypeStruct((m, n), F32),
        scratch_shapes=[pltpu.VMEM((tm, tn), F32)],
        compiler_params=_cparams(("parallel", "parallel", "arbitrary")),
        name="mlp_down",
    )(a, w, resid)


def _cumsum_kernel(x_ref, o_ref, carry_ref):
    c = pl.program_id(1)

    @pl.when(c == 0)
    def _():
        carry_ref[...] = jnp.zeros_like(carry_ref)

    n = x_ref.shape[1]
    row = lax.broadcasted_iota(jnp.int32, (n, n), 0)
    col = lax.broadcasted_iota(jnp.int32, (n, n), 1)
    tri = jnp.where(col <= row, 1.0, 0.0).astype(BF16)
    acc = carry_ref[...]
    for piece in _split_bf16(x_ref[0], 3):
        acc = acc + _dot(tri, piece)
    o_ref[0] = acc
    carry_ref[...] = jnp.broadcast_to(acc[n - 1:n, :], carry_ref.shape)


def _cumsum_rows(x):
    r, t, lanes = x.shape
    n = 128
    return pl.pallas_call(
        _cumsum_kernel,
        grid=(r, t // n),
        in_specs=[pl.BlockSpec((1, n, lanes), lambda i, c: (i, c, 0))],
        out_specs=pl.BlockSpec((1, n, lanes), lambda i, c: (i, c, 0)),
        out_shape=jax.ShapeDtypeStruct((r, t, lanes), F32),
        scratch_shapes=[pltpu.VMEM((n, lanes), F32)],
        compiler_params=_cparams(("parallel", "arbitrary")),
        name="logf_cumsum",
    )(x)


def _gather_logf_kernel(pt_ref, *refs):
    o_ref = refs[-1]
    for p, x_ref in enumerate(refs[:-1]):
        o_ref[0, p] = x_ref[0, 0]


def _gather_logf(cache_logf, page_table, layer):
    bs, n_pages = page_table.shape
    depth, n_pool, page, h = cache_logf.shape
    flat = cache_logf.reshape(depth, n_pool, 1, page * h)
    in_specs = [pl.BlockSpec((1, 1, 1, page * h), functools.partial(lambda b, pt, p: (layer, pt[b, p], 0, 0), p=p))
                for p in range(n_pages)]
    out = pl.pallas_call(
        _gather_logf_kernel,
        grid_spec=pltpu.PrefetchScalarGridSpec(
            num_scalar_prefetch=1, grid=(bs,), in_specs=in_specs,
            out_specs=pl.BlockSpec((1, n_pages, 1, page * h), lambda b, pt: (b, 0, 0, 0))),
        out_shape=jax.ShapeDtypeStruct((bs, n_pages, 1, page * h), F32),
        compiler_params=_cparams(("arbitrary",)),
        name="gather_logf",
    )(page_table, *([flat] * n_pages))
    return out.reshape(bs, n_pages, page, h)


def _rope(x, cosf, sinf):
    return x * cosf + pltpu.roll(x, HEAD_DIM // 2, 1) * sinf


def _log_gamma(h):
    return math.log(1.0 - 2.0 ** (-5.0 - h))


def _head_norm_gate(o, g):
    on = o * lax.rsqrt(jnp.mean(o * o, axis=-1, keepdims=True) + EPS)
    return on * (g * _sigmoid(g))


def _ret_prompt_kernel(q_ref, k_ref, v_ref, g_ref, cos_ref, sin_ref, o_ref, s_ref, *, n_heads):
    c = q_ref.shape[0]

    @pl.when(pl.program_id(1) == 0)
    def _():
        s_ref[...] = jnp.zeros_like(s_ref)

    cosf, sinf = cos_ref[...], sin_ref[...]
    ii = lax.broadcasted_iota(jnp.int32, (c, c), 0).astype(F32)
    jj = lax.broadcasted_iota(jnp.int32, (c, c), 1).astype(F32)
    diff = ii - jj
    scale = HEAD_DIM ** -0.5
    for h in range(n_heads):
        lg = _log_gamma(h)
        hs = slice(h * HEAD_DIM, (h + 1) * HEAD_DIM)
        inner = jnp.where(diff >= 0, jnp.exp(diff * lg), 0.0)
        cross = jnp.exp((ii + 1.0) * lg)
        kdec = jnp.exp((c - 1.0 - ii) * lg)
        qr = _rope(q_ref[:, hs], cosf, sinf)
        kr = _rope(k_ref[:, hs], cosf, sinf) * scale
        qb, kb, vb = qr.astype(BF16), kr.astype(BF16), v_ref[:, hs].astype(BF16)
        a = _dot_nt(qb, kb) * inner
        s = s_ref[0, h]
        o = _dot(a.astype(BF16), vb) + _dot(qb, s.astype(BF16)) * cross
        s_ref[0, h] = s * math.exp(c * lg) + _dot((kr * kdec).T.astype(BF16), vb)
        o_ref[:, hs] = _head_norm_gate(o, g_ref[:, hs]).astype(o_ref.dtype)


def _ret_prompt(proj, cosf, sinf, batch, seq, n_heads, col0):
    mix = n_heads * HEAD_DIM
    c = 128
    nc = seq // c
    cb = col0 // mix

    def seg(r):
        return pl.BlockSpec((c, mix), lambda b, i: (b * nc + i, cb + r))

    tab = pl.BlockSpec((c, HEAD_DIM), lambda b, i: (i, 0))
    return pl.pallas_call(
        functools.partial(_ret_prompt_kernel, n_heads=n_heads),
        grid=(batch, nc),
        in_specs=[seg(0), seg(1), seg(2), seg(3), tab, tab],
        out_specs=[pl.BlockSpec((c, mix), lambda b, i: (b * nc + i, 0)),
                   pl.BlockSpec((1, n_heads, HEAD_DIM, HEAD_DIM), lambda b, i: (b, 0, 0, 0))],
        out_shape=[jax.ShapeDtypeStruct((batch * seq, mix), BF16),
                   jax.ShapeDtypeStruct((batch, n_heads, HEAD_DIM, HEAD_DIM), F32)],
        compiler_params=_cparams(("parallel", "arbitrary")),
        name="retention_prompt",
    )(proj, proj, proj, proj, cosf, sinf)


def _ret_sample_kernel(q_ref, k_ref, v_ref, g_ref, cos_ref, sin_ref, s0_ref, o_ref, s_ref, *, t_real):
    n = q_ref.shape[0]
    hf = jnp.full((n, n), pl.program_id(1), jnp.int32).astype(F32)
    lg = jnp.log(1.0 - jnp.exp2(-5.0 - hf))
    ri = lax.broadcasted_iota(jnp.int32, (n, n), 0)
    ci = lax.broadcasted_iota(jnp.int32, (n, n), 1)
    rseq, cseq = _div_pow2(ri, SAMPLE_ROWS), _div_pow2(ci, SAMPLE_ROWS)
    il, jl = ri & (SAMPLE_ROWS - 1), ci & (SAMPLE_ROWS - 1)
    ilf = il.astype(F32)
    causal = jnp.logical_and(rseq == cseq, jnp.logical_and(jl <= il, il < t_real))
    inner = jnp.where(causal, jnp.exp((il - jl).astype(F32) * lg), 0.0)
    cross = jnp.exp((ilf + 1.0) * lg)
    kdec = jnp.where(il < t_real, jnp.exp((t_real - 1.0 - ilf) * lg), 0.0)
    cdec = jnp.exp(float(t_real) * lg)
    cosf, sinf = cos_ref[...], sin_ref[...]
    qr = _rope(q_ref[...], cosf, sinf)
    kr = _rope(k_ref[...], cosf, sinf) * (HEAD_DIM ** -0.5)
    qb, kb, vb = qr.astype(BF16), kr.astype(BF16), v_ref[...].astype(BF16)
    o = _dot((_dot_nt(qb, kb) * inner).astype(BF16), vb)
    kdt = (kr * kdec).T
    for sq in range(n // SAMPLE_ROWS):
        s = s0_ref[sq, 0]
        o = jnp.where(rseq == sq, o + _dot(qb, s.astype(BF16)) * cross, o)
        s_ref[sq, 0] = s * cdec + _dot(jnp.where(cseq == sq, kdt, 0.0).astype(BF16), vb)
    o_ref[...] = _head_norm_gate(o, g_ref[...]).astype(o_ref.dtype)


def _ret_sample(proj, cosf, sinf, s0, row0, t_real, col0):
    bs, n_heads = s0.shape[:2]
    mix = n_heads * HEAD_DIM
    n = SEQ_PER_GROUP * SAMPLE_ROWS
    rb = row0 // n
    cb = col0 // HEAD_DIM

    def seg(r):
        return pl.BlockSpec((n, HEAD_DIM), lambda gidx, h: (rb + gidx, cb + r * n_heads + h))

    tab = pl.BlockSpec((n, HEAD_DIM), lambda gidx, h: (0, 0))
    st = pl.BlockSpec((SEQ_PER_GROUP, 1, HEAD_DIM, HEAD_DIM), lambda gidx, h: (gidx, h, 0, 0))
    return pl.pallas_call(
        functools.partial(_ret_sample_kernel, t_real=t_real),
        grid=(bs // SEQ_PER_GROUP, n_heads),
        in_specs=[seg(0), seg(1), seg(2), seg(3), tab, tab, st],
        out_specs=[pl.BlockSpec((n, HEAD_DIM), lambda gidx, h: (gidx, h)), st],
        out_shape=[jax.ShapeDtypeStruct((bs * SAMPLE_ROWS, mix), BF16),
                   jax.ShapeDtypeStruct(s0.shape, F32)],
        compiler_params=_cparams(("parallel", "arbitrary")),
        name="retention_sample",
    )(proj, proj, proj, proj, cosf, sinf, s0)


def _pair_tables(nq, descending):
    qi, ki = [], []
    for q in range(nq):
        ks = range(q, -1, -1) if descending else range(q + 1)
        for k in ks:
            qi.append(q)
            ki.append(k)
    return jnp.asarray(qi, jnp.int32), jnp.asarray(ki, jnp.int32)


def _fox_prompt_kernel(qi_ref, ki_ref, q_ref, k_ref, v_ref, nc_ref, o_ref, m_ref, l_ref, acc_ref, *, n_heads):
    p = pl.program_id(1)
    qi, ki = qi_ref[p], ki_ref[p]
    tq, tk = q_ref.shape[0], k_ref.shape[0]

    @pl.when(ki == 0)
    def _():
        m_ref[...] = jnp.full_like(m_ref, -jnp.inf)
        l_ref[...] = jnp.zeros_like(l_ref)
        acc_ref[...] = jnp.zeros_like(acc_ref)

    scale = HEAD_DIM ** -0.5

    rows = qi * tq + lax.broadcasted_iota(jnp.int32, (tq, tk), 0)
    cols = ki * tk + lax.broadcasted_iota(jnp.int32, (tq, tk), 1)
    valid = cols <= rows
    for h in range(n_heads):
        hs = slice(h * HEAD_DIM, (h + 1) * HEAD_DIM)
        qb, kb, vb = q_ref[:, hs].astype(BF16), k_ref[:, hs].astype(BF16), v_ref[:, hs].astype(BF16)
        s = _dot_nt(qb, kb) * scale + nc_ref[0, h]
        s = jnp.where(valid, s, -jnp.inf)
        m_prev = m_ref[h]
        m_new = jnp.maximum(m_prev, jnp.max(s, axis=-1, keepdims=True))
        alpha = jnp.exp(m_prev - m_new)
        pr = jnp.exp(s - m_new)
        l_ref[h] = alpha * l_ref[h] + jnp.sum(pr, axis=-1, keepdims=True)
        acc_ref[h] = alpha * acc_ref[h] + _dot(pr.astype(BF16), vb)
        m_ref[h] = m_new

    @pl.when(ki == qi)
    def _():
        for h in range(n_heads):
            o_ref[:, h * HEAD_DIM:(h + 1) * HEAD_DIM] = (acc_ref[h] / l_ref[h]).astype(o_ref.dtype)


def _sb_prompt_kernel(qi_ref, ki_ref, q_ref, k_ref, v_ref, o_ref, carry_ref, acc_ref, *, n_heads):
    p = pl.program_id(1)
    qi, ki = qi_ref[p], ki_ref[p]
    tq, tk = q_ref.shape[0], k_ref.shape[0]
    sub = min(CUMSUM_SUB, tk)

    @pl.when(ki == qi)
    def _():
        carry_ref[...] = jnp.zeros_like(carry_ref)
        acc_ref[...] = jnp.zeros_like(acc_ref)

    ur = lax.broadcasted_iota(jnp.int32, (sub, sub), 0)
    uc = lax.broadcasted_iota(jnp.int32, (sub, sub), 1)
    upper = jnp.where(ur >= uc, 1.0, 0.0).astype(BF16)
    scale = HEAD_DIM ** -0.5

    def step(diagonal):
        if diagonal:
            rows = lax.broadcasted_iota(jnp.int32, (tq, sub), 0)
            col0 = lax.broadcasted_iota(jnp.int32, (tq, sub), 1)
        for h in range(n_heads):
            hs = slice(h * HEAD_DIM, (h + 1) * HEAD_DIM)
            qb = q_ref[:, hs].astype(BF16)
            carry = carry_ref[h]
            acc = acc_ref[h]
            for sb in range(tk // sub - 1, -1, -1):
                ks = slice(sb * sub, (sb + 1) * sub)
                kb, vb = k_ref[ks, hs].astype(BF16), v_ref[ks, hs].astype(BF16)
                z = _dot_nt(qb, kb) * scale
                lneg = -(jnp.maximum(z, 0.0) + jnp.log(1.0 + jnp.exp(-jnp.abs(z))))
                if diagonal:
                    valid = col0 + sb * sub < rows
                    lneg = jnp.where(valid, lneg, 0.0)
                hi, lo = _split_bf16(lneg, 2)
                incl = _dot(hi, upper) + _dot(lo, upper)
                ex = z + incl + carry
                if diagonal:
                    ex = jnp.where(valid, ex, -1e30)
                acc = acc + _dot(jnp.exp(ex).astype(BF16), vb)
                carry = carry + incl[:, 0:1]
            carry_ref[h] = carry
            acc_ref[h] = acc

    @pl.when(ki == qi)
    def _():
        step(True)

    @pl.when(ki < qi)
    def _():
        step(False)

    @pl.when(ki == 0)
    def _():
        for h in range(n_heads):
            o_ref[:, h * HEAD_DIM:(h + 1) * HEAD_DIM] = acc_ref[h].astype(o_ref.dtype)


def _prompt_attention(kind, proj, batch, seq, n_heads, col0, neg_cum=None):
    mix = n_heads * HEAD_DIM
    t = _tile(seq, 512, 128)
    nq = seq // t
    cb = col0 // mix
    qi, ki = _pair_tables(nq, descending=(kind == "sb"))
    q_spec = pl.BlockSpec((t, mix), lambda b, p, qt, kt: (b * nq + qt[p], cb))
    k_spec = pl.BlockSpec((t, mix), lambda b, p, qt, kt: (b * nq + kt[p], cb + 1))
    v_spec = pl.BlockSpec((t, mix), lambda b, p, qt, kt: (b * nq + kt[p], cb + 2))
    o_spec = pl.BlockSpec((t, mix), lambda b, p, qt, kt: (b * nq + qt[p], 0))
    col_state = pltpu.VMEM((n_heads, t, 1), F32)
    acc_state = pltpu.VMEM((n_heads, t, HEAD_DIM), F32)
    if kind == "fox":
        body = functools.partial(_fox_prompt_kernel, n_heads=n_heads)
        in_specs = [q_spec, k_spec, v_spec,
                    pl.BlockSpec((1, n_heads, 1, t), lambda b, p, qt, kt: (b, 0, 0, kt[p]))]
        args = (proj, proj, proj, neg_cum)
        scratch = [col_state, col_state, acc_state]
    else:
        body = functools.partial(_sb_prompt_kernel, n_heads=n_heads)
        in_specs = [q_spec, k_spec, v_spec]
        args = (proj, proj, proj)
        scratch = [col_state, acc_state]
    return pl.pallas_call(
        body,
        grid_spec=pltpu.PrefetchScalarGridSpec(
            num_scalar_prefetch=2, grid=(batch, int(qi.shape[0])), in_specs=in_specs, out_specs=o_spec,
            scratch_shapes=scratch),
        out_shape=jax.ShapeDtypeStruct((batch * seq, mix), BF16),
        compiler_params=_cparams(("parallel", "arbitrary")),
        name=kind + "_prompt",
    )(qi, ki, *args)


def _page_heads(refs, n_heads):
    page = refs[0].shape[2] // n_heads
    out = []
    for h in range(n_heads):
        rows = [r[0, 0, pl.ds(h, page, stride=n_heads), :] for r in refs]
        out.append(jnp.concatenate(rows, axis=0).astype(BF16))
    return out


def _new_heads(x_ref, pad_ref, n_heads):
    pad_ref[0:SAMPLE_ROWS, :] = x_ref[...]
    return [pad_ref[:, h * HEAD_DIM:(h + 1) * HEAD_DIM].astype(BF16) for h in range(n_heads)]


def _stacked_scores(q_ref, k_heads):
    z = [_dot_nt(q_ref[0, h].astype(BF16), kb)[0:SAMPLE_ROWS] for h, kb in enumerate(k_heads)]
    return jnp.concatenate(z, axis=0) * (HEAD_DIM ** -0.5)


def _stacked_values(p, v_heads):
    zeros = jnp.zeros((Q_ROWS - SAMPLE_ROWS, p.shape[1]), F32)
    out = []
    for h, vb in enumerate(v_heads):
        ph = jnp.concatenate([p[h * SAMPLE_ROWS:(h + 1) * SAMPLE_ROWS], zeros], axis=0).astype(BF16)
        out.append(_dot(ph, vb)[0:SAMPLE_ROWS])
    return jnp.concatenate(out, axis=0)


def _new_key_mask(n_rows, page, t_real, strict):
    qidx = lax.broadcasted_iota(jnp.int32, (n_rows, page), 0) & (SAMPLE_ROWS - 1)
    kidx = lax.broadcasted_iota(jnp.int32, (n_rows, page), 1)
    causal = kidx < qidx if strict else kidx <= qidx
    return jnp.logical_and(causal, kidx < t_real)


def _store_heads(o_ref, out, t_real, n_heads):
    real = (lax.broadcasted_iota(jnp.int32, out.shape, 0) & (SAMPLE_ROWS - 1)) < t_real
    out = jnp.where(real, out, 0.0)
    for h in range(n_heads):
        o_ref[0, :, h * HEAD_DIM:(h + 1) * HEAD_DIM] = out[h * SAMPLE_ROWS:(h + 1) * SAMPLE_ROWS]


def _fox_sample_kernel(pt_ref, q_ref, kn_ref, vn_ref, nc_ref, ncn_ref, *refs, n_heads, t_real):
    k_pages, v_pages = refs[:PAGES_PER_STEP], refs[PAGES_PER_STEP:2 * PAGES_PER_STEP]
    o_ref, m_ref, l_ref, acc_ref, kpad_ref, vpad_ref = refs[2 * PAGES_PER_STEP:]
    g = pl.program_id(1)
    n_rows = n_heads * SAMPLE_ROWS

    @pl.when(jnp.logical_and(pl.program_id(0) == 0, g == 0))
    def _():
        kpad_ref[...] = jnp.zeros_like(kpad_ref)
        vpad_ref[...] = jnp.zeros_like(vpad_ref)

    @pl.when(g == 0)
    def _():
        m_ref[...] = jnp.full_like(m_ref, -jnp.inf)
        l_ref[...] = jnp.zeros_like(l_ref)
        acc_ref[...] = jnp.zeros_like(acc_ref)

    def block(k_heads, v_heads, bias_ref, valid):
        nk = bias_ref.shape[3]
        bias = jnp.concatenate([jnp.broadcast_to(bias_ref[0, h], (SAMPLE_ROWS, nk)) for h in range(n_heads)], axis=0)
        s = _stacked_scores(q_ref, k_heads) + bias
        if valid is not None:
            s = jnp.where(valid, s, -jnp.inf)
        m_prev = m_ref[...]
        m_new = jnp.maximum(m_prev, jnp.max(s, axis=-1, keepdims=True))
        alpha = jnp.exp(m_prev - m_new)
        pr = jnp.exp(s - m_new)
        l_ref[...] = alpha * l_ref[...] + jnp.sum(pr, axis=-1, keepdims=True)
        acc_ref[...] = alpha * acc_ref[...] + _stacked_values(pr, v_heads)
        m_ref[...] = m_new

    block(_page_heads(k_pages, n_heads), _page_heads(v_pages, n_heads), nc_ref, None)

    @pl.when(g == pl.num_programs(1) - 1)
    def _():
        page = kpad_ref.shape[0]
        block(_new_heads(kn_ref, kpad_ref, n_heads), _new_heads(vn_ref, vpad_ref, n_heads), ncn_ref,
              _new_key_mask(n_rows, page, t_real, strict=False))
        _store_heads(o_ref, acc_ref[...] / l_ref[...], t_real, n_heads)


def _sb_sample_kernel(pt_ref, q_ref, kn_ref, vn_ref, *refs, n_heads, t_real):
    k_pages, v_pages = refs[:PAGES_PER_STEP], refs[PAGES_PER_STEP:2 * PAGES_PER_STEP]
    o_ref, carry_ref, acc_ref, kpad_ref, vpad_ref = refs[2 * PAGES_PER_STEP:]
    g = pl.program_id(1)
    n_rows = n_heads * SAMPLE_ROWS

    @pl.when(jnp.logical_and(pl.program_id(0) == 0, g == 0))
    def _():
        kpad_ref[...] = jnp.zeros_like(kpad_ref)
        vpad_ref[...] = jnp.zeros_like(vpad_ref)

    def block(k_heads, v_heads, valid, sub):
        z = _stacked_scores(q_ref, k_heads)
        nsub = z.shape[1] // sub
        ur = lax.broadcasted_iota(jnp.int32, (sub, sub), 0)
        uc = lax.broadcasted_iota(jnp.int32, (sub, sub), 1)
        upper = jnp.where(ur >= uc, 1.0, 0.0).astype(BF16)
        lneg = -(jnp.maximum(z, 0.0) + jnp.log(1.0 + jnp.exp(-jnp.abs(z))))
        if valid is not None:
            lneg = jnp.where(valid, lneg, 0.0)
        hi, lo = _split_bf16(lneg, 2)
        incl = []
        for j in range(nsub):
            js = slice(j * sub, (j + 1) * sub)
            r = _dot(jnp.concatenate([hi[:, js], lo[:, js]], axis=0), upper)
            incl.append(r[:n_rows] + r[n_rows:])
        carry = carry_ref[...]
        a = [None] * nsub
        for j in range(nsub - 1, -1, -1):
            ex = z[:, j * sub:(j + 1) * sub] + incl[j] + carry
            if valid is not None:
                ex = jnp.where(valid[:, j * sub:(j + 1) * sub], ex, -1e30)
            a[j] = jnp.exp(ex)
            carry = carry + incl[j][:, 0:1]
        carry_ref[...] = carry
        acc_ref[...] += _stacked_values(jnp.concatenate(a, axis=1), v_heads)

    @pl.when(g == 0)
    def _():
        carry_ref[...] = jnp.zeros_like(carry_ref)
        acc_ref[...] = jnp.zeros_like(acc_ref)
        page = kpad_ref.shape[0]
        block(_new_heads(kn_ref, kpad_ref, n_heads), _new_heads(vn_ref, vpad_ref, n_heads),
              _new_key_mask(n_rows, page, t_real, strict=True), page)

    block(_page_heads(k_pages, n_heads), _page_heads(v_pages, n_heads), None, CUMSUM_SUB)

    @pl.when(g == pl.num_programs(1) - 1)
    def _():
        _store_heads(o_ref, acc_ref[...], t_real, n_heads)


def _sample_attention(kind, q, proj, cache_k, cache_v, page_table, layer, row0, col0, t_real, neg_cum=None):
    bs, n_pages = page_table.shape
    depth, n_pool, page, n_heads, _ = cache_k.shape
    mix = n_heads * HEAD_DIM
    n_rows = n_heads * SAMPLE_ROWS
    n_steps = n_pages // PAGES_PER_STEP
    assert n_pages % PAGES_PER_STEP == 0 and (PAGES_PER_STEP * page) % CUMSUM_SUB == 0
    ck = cache_k.reshape(depth, n_pool, page * n_heads, HEAD_DIM)
    cv = cache_v.reshape(depth, n_pool, page * n_heads, HEAD_DIM)
    rb = row0 // SAMPLE_ROWS
    cb = col0 // mix
    descending = kind == "sb"

    def page_spec(pp):
        def imap(b, g, pt):
            gg = n_steps - 1 - g if descending else g
            return (layer, pt[b, gg * PAGES_PER_STEP + pp], 0, 0)
        return pl.BlockSpec((1, 1, page * n_heads, HEAD_DIM), imap)

    q_spec = pl.BlockSpec((1, n_heads, Q_ROWS, HEAD_DIM), lambda b, g, pt: (b, 0, 0, 0))
    kn_spec = pl.BlockSpec((SAMPLE_ROWS, mix), lambda b, g, pt: (rb + b, cb + 1))
    vn_spec = pl.BlockSpec((SAMPLE_ROWS, mix), lambda b, g, pt: (rb + b, cb + 2))
    page_specs = [page_spec(pp) for pp in range(PAGES_PER_STEP)]
    o_spec = pl.BlockSpec((1, SAMPLE_ROWS, mix), lambda b, g, pt: (b, 0, 0))
    col_state = pltpu.VMEM((n_rows, 1), F32)
    acc_state = pltpu.VMEM((n_rows, HEAD_DIM), F32)
    pad_state = pltpu.VMEM((page, mix), F32)
    if kind == "fox":
        body = functools.partial(_fox_sample_kernel, n_heads=n_heads, t_real=t_real)
        in_specs = [q_spec, kn_spec, vn_spec,
                    pl.BlockSpec((1, n_heads, 1, PAGES_PER_STEP * page), lambda b, g, pt: (b, 0, 0, g)),
                    pl.BlockSpec((1, n_heads, 1, page), lambda b, g, pt: (b, 0, 0, n_pages))]
        args = (q, proj, proj, neg_cum, neg_cum)
        scratch = [col_state, col_state, acc_state, pad_state, pad_state]
    else:
        body = functools.partial(_sb_sample_kernel, n_heads=n_heads, t_real=t_real)
        in_specs = [q_spec, kn_spec, vn_spec]
        args = (q, proj, proj)
        scratch = [col_state, acc_state, pad_state, pad_state]
    return pl.pallas_call(
        body,
        grid_spec=pltpu.PrefetchScalarGridSpec(
            num_scalar_prefetch=1, grid=(bs, n_steps), in_specs=in_specs + page_specs + page_specs,
            out_specs=o_spec, scratch_shapes=scratch),
        out_shape=jax.ShapeDtypeStruct((bs, SAMPLE_ROWS, mix), F32),
        compiler_params=_cparams(("arbitrary", "arbitrary")),
        name=kind + "_sample",
    )(page_table, *args, *([ck] * PAGES_PER_STEP), *([cv] * PAGES_PER_STEP))


def _rope_tables(pos):
    half = HEAD_DIM // 2
    inv = ROPE_BASE ** (-jnp.arange(half, dtype=F32) / half)
    ang = pos.astype(F32)[:, None] * inv[None, :]
    cos, sin = jnp.cos(ang), jnp.sin(ang)
    return jnp.concatenate([cos, cos], axis=-1), jnp.concatenate([-sin, sin], axis=-1)


def _head_major_queries(q, t_real):
    qt = jnp.transpose(q[:, :t_real], (0, 2, 1, 3))
    return jnp.pad(qt, ((0, 0), (0, 0), (0, Q_ROWS - t_real), (0, 0)))


def kernel(x_prompt, x_sample, cache_fox_k, cache_fox_v, cache_fox_logf, cache_sb_k, cache_sb_v, state_ret,
           page_table, w_in, b_gate, b_forget, w_ret_o, w_fox_o, w_sb_o, w_out, w_up, w_down, g_mix, g_mlp,
           g_final):
    bp, seq, d = x_prompt.shape
    bs, t_real, _ = x_sample.shape
    depth, _, page, n_heads, _ = cache_fox_k.shape
    n_pages = page_table.shape[1]
    past = n_pages * page
    mix = n_heads * HEAD_DIM
    m_p = bp * seq
    m_s = bs * SAMPLE_ROWS
    assert t_real <= SAMPLE_ROWS and bs % SEQ_PER_GROUP == 0 and page == 128 and seq % 128 == 0
    c_ret, c_fox, c_sb, c_gate = 0, 4 * mix, 7 * mix, 10 * mix
    f0 = 7 * mix

    xs = jnp.pad(x_sample, ((0, 0), (0, SAMPLE_ROWS - t_real), (0, 0)))
    x = jnp.concatenate([x_prompt.reshape(m_p, d), xs.reshape(m_s, d)], axis=0)

    cos_p, sin_p = _rope_tables(jnp.arange(seq))
    cos_s, sin_s = _rope_tables(past + jnp.arange(SAMPLE_ROWS))
    cos_s, sin_s = jnp.tile(cos_s, (SEQ_PER_GROUP, 1)), jnp.tile(sin_s, (SEQ_PER_GROUP, 1))
    lane_pad = 128 - n_heads

    outs_p = [[] for _ in range(6)]
    outs_s = [[] for _ in range(6)]
    for layer in range(depth):
        w_main = jnp.concatenate([w_in[layer, :, :f0], w_in[layer, :, f0 + n_heads:]], axis=1).astype(BF16)
        w_fg = jnp.pad(w_in[layer, :, f0:f0 + n_heads], ((0, 0), (0, lane_pad))).astype(BF16)
        b_fg = jnp.pad(b_forget[layer], (0, lane_pad)).reshape(1, 128)

        xn = _rmsnorm(x, g_mix[layer], BF16)
        proj = _matmul(xn, w_main, F32)
        logf = _forget_proj(xn, w_fg, b_fg)

        o_r_p, ret_p = _ret_prompt(proj, cos_p, sin_p, bp, seq, n_heads, c_ret)
        o_r_s, ret_s = _ret_sample(proj, cos_s, sin_s, state_ret[layer], m_p, t_real, c_ret)

        cum_p = _cumsum_rows(logf[:m_p].reshape(bp, seq, 128))
        neg_cum_p = -jnp.transpose(cum_p[:, :, :n_heads], (0, 2, 1)).reshape(bp, n_heads, 1, seq)
        o_f_p = _prompt_attention("fox", proj, bp, seq, n_heads, c_fox, neg_cum_p)

        logf_s = logf[m_p:].reshape(bs, SAMPLE_ROWS, 128)[:, :, :n_heads]
        logf_s = jnp.where(jnp.arange(SAMPLE_ROWS)[None, :, None] < t_real, logf_s, 0.0)
        logf_past = _gather_logf(cache_fox_logf, page_table, layer).reshape(bs, past, n_heads)
        total = past + page
        lf = jnp.concatenate([logf_past, logf_s, jnp.zeros((bs, page - SAMPLE_ROWS, n_heads), F32)], axis=1)
        n_grp = bs * n_heads // 128
        lf = jnp.transpose(lf.reshape(n_grp, 128 // n_heads, total, n_heads), (0, 2, 1, 3)).reshape(n_grp, total, 128)
        cum_s = _cumsum_rows(lf)
        cum_s = jnp.transpose(cum_s.reshape(n_grp, total, 128 // n_heads, n_heads), (0, 2, 3, 1))
        neg_cum_s = -cum_s.reshape(bs, n_heads, 1, total)
        q_f_s = proj[m_p:, c_fox:c_fox + mix].reshape(bs, SAMPLE_ROWS, n_heads, HEAD_DIM)
        o_f_s = _sample_attention("fox", _head_major_queries(q_f_s, t_real), proj, cache_fox_k, cache_fox_v,
                                  page_table, layer, m_p, c_fox, t_real, neg_cum_s)

        o_s_p = _prompt_attention("sb", proj, bp, seq, n_heads, c_sb)
        q_s_s = proj[m_p:, c_sb:c_sb + mix].reshape(bs, SAMPLE_ROWS, n_heads, HEAD_DIM)
        o_s_s = _sample_attention("sb", _head_major_queries(q_s_s, t_real), proj, cache_sb_k, cache_sb_v,
                                  page_table, layer, m_p, c_sb, t_real)

        o_r = jnp.concatenate([o_r_p, o_r_s], axis=0)
        o_f = jnp.concatenate([o_f_p, o_f_s.reshape(m_s, mix).astype(BF16)], axis=0)
        o_s = jnp.concatenate([o_s_p, o_s_s.reshape(m_s, mix).astype(BF16)], axis=0)

        merged = _merge(o_r, o_f, o_s, w_ret_o[layer].astype(BF16), w_fox_o[layer].astype(BF16),
                        w_sb_o[layer].astype(BF16), proj, c_gate, b_gate[layer])
        x = _matmul_resid(merged, w_out[layer].astype(BF16), x)
        hmid = _mlp_up(_rmsnorm(x, g_mlp[layer], BF16), w_up[layer].astype(BF16))
        x = _mlp_down(hmid, w_down[layer].astype(BF16), x)

        def seg_p(c0):
            return proj[:m_p, c0:c0 + mix].reshape(bp, seq, n_heads, HEAD_DIM)

        def seg_s(c0):
            return proj[m_p:, c0:c0 + mix].reshape(bs, SAMPLE_ROWS, n_heads, HEAD_DIM)[:, :t_real]

        for lst, val in zip(outs_p, (seg_p(c_fox + mix), seg_p(c_fox + 2 * mix),
                                     logf[:m_p, :n_heads].reshape(bp, seq, n_heads),
                                     seg_p(c_sb + mix), seg_p(c_sb + 2 * mix), ret_p)):
            lst.append(val)
        for lst, val in zip(outs_s, (seg_s(c_fox + mix), seg_s(c_fox + 2 * mix),
                                     logf[m_p:, :n_heads].reshape(bs, SAMPLE_ROWS, n_heads)[:, :t_real],
                                     seg_s(c_sb + mix), seg_s(c_sb + 2 * mix), ret_s)):
            lst.append(val)

    y = _rmsnorm(x, g_final, F32)
    y_prompt = y[:m_p].reshape(bp, seq, d)
    y_sample = y[m_p:].reshape(bs, SAMPLE_ROWS, d)[:, :t_real]
    return (y_prompt, y_sample, *[jnp.stack(v, axis=0) for v in outs_p], *[jnp.stack(v, axis=0) for v in outs_s])
```
